```python
import math
import jax, jax.numpy as jnp
from jax import lax
import numpy as np

D_MODEL = 1024
BATCH = 8
SEQ = 8192
DEPTH = 2
DEC_BATCH = 2
DEC_SEQ = 8192
PAST_LEN = 128

ATTN_WIDTH = 512
LRU_WIDTH = 512
N_DIFF_HEADS = 4
QK_HEAD_DIM = 64
V_HEAD_DIM = 2 * QK_HEAD_DIM
ROT_DIM = QK_HEAD_DIM // 4
ROPE_THETA = 500000.0
Q_BLOCK = 128
CONV_WIDTH = 4
CONV_LEFT = 2
LRU_BLOCKS = 8
LRU_BLOCK_DIM = LRU_WIDTH // LRU_BLOCKS
LRU_C = 8.0
PEER_HEADS = 8
N_KEYS = 128
N_EXPERTS = N_KEYS * N_KEYS
PEER_TOPK = 16
PEER_HALF = 128
PEER_QUERY_DIM = 2 * PEER_HALF
PEER_CHUNK = 128
IN_WIDTH = 3 * ATTN_WIDTH + 2 * LRU_WIDTH
DN_ALPHA = (2 * DEPTH) ** 0.25
DN_BETA = (8 * DEPTH) ** -0.25
LN_EPS = 1e-5

kernel_name = "hymba_diffattn_rglru_peer_encoder"

F32 = jnp.float32


def layer_norm(x, g, b):
    xf = x.astype(F32)
    mu = jnp.mean(xf, axis=-1, keepdims=True)
    var = jnp.mean(jnp.square(xf - mu), axis=-1, keepdims=True)
    return (xf - mu) * lax.rsqrt(var + LN_EPS) * g.astype(F32) + b.astype(F32)


def rotary_tables(seq):
    inv = ROPE_THETA ** (-jnp.arange(0, ROT_DIM, 2, dtype=F32) / ROT_DIM)
    ang = jnp.arange(seq, dtype=F32)[:, None] * inv[None, :]
    return jnp.cos(ang), jnp.sin(ang)


def partial_rotary(t, cos, sin):
    half = ROT_DIM // 2
    r1 = t[..., :half]
    r2 = t[..., half:ROT_DIM]
    c = cos[None, :, None, None, :]
    s = sin[None, :, None, None, :]
    return jnp.concatenate([r1 * c - r2 * s, r2 * c + r1 * s, t[..., ROT_DIM:]], axis=-1)


def diff_attention(q, k, v, lam, subln_g, lambda_init):
    B, S = q.shape[0], q.shape[1]
    nq = S // Q_BLOCK
    qb = jnp.moveaxis(q.reshape(B, nq, Q_BLOCK, N_DIFF_HEADS, 2, QK_HEAD_DIM), 1, 0)
    scale = QK_HEAD_DIM ** -0.5
    kf = k.astype(F32)
    vf = v.astype(F32)

    def block(qc):
        s = jnp.einsum('bqhcd,bkhcd->bhcqk', qc.astype(F32), kf) * scale
        p = jax.nn.softmax(s, axis=-1)
        a = p[:, :, 0] - lam * p[:, :, 1]
        return jnp.einsum('bhqk,bkhe->bqhe', a, vf)

    o = lax.map(block, qb)
    o = jnp.moveaxis(o, 0, 1).reshape(B, S, N_DIFF_HEADS, V_HEAD_DIM)
    o = o * lax.rsqrt(jnp.mean(o * o, axis=-1, keepdims=True) + 1e-5) * subln_g.astype(F32)
    return (o * (1.0 - lambda_init)).reshape(B, S, ATTN_WIDTH)


def centred_dwconv(x, w, b):
    S = x.shape[1]
    xp = jnp.pad(x, ((0, 0), (CONV_LEFT, CONV_WIDTH - 1 - CONV_LEFT), (0, 0)))
    out = xp[:, 0:S] * w[0]
    for j in range(1, CONV_WIDTH):
        out = out + xp[:, j:j + S] * w[j]
    return out + b


def rg_lru(x, w_a, b_a, w_i, b_i, lam, reverse):
    B, S, W = x.shape
    xb = x.reshape(B, S, LRU_BLOCKS, LRU_BLOCK_DIM)
    r = jax.nn.sigmoid(jnp.einsum('bsnd,nde->bsne', xb, w_a).reshape(B, S, W) + b_a)
    i = jax.nn.sigmoid(jnp.einsum('bsnd,nde->bsne', xb, w_i).reshape(B, S, W) + b_i)
    log_a = (-LRU_C * r * jax.nn.softplus(-lam.astype(F32))).astype(F32)
    a = jnp.exp(log_a)
    bt = (jnp.sqrt(-jnp.expm1(2.0 * log_a)) * (i * x)).astype(F32)

    def combine(lhs, rhs):
        a1, b1 = lhs
        a2, b2 = rhs
        return a1 * a2, a2 * b1 + b2

    _, h = lax.associative_scan(combine, (a, bt), axis=1, reverse=reverse)
    return h


def recurrent_mixer(xr, gate, conv_w, conv_b, ga_w, ga_b, gi_w, gi_b, lru_lam):
    xc = centred_dwconv(xr.astype(F32), conv_w.astype(F32), conv_b.astype(F32))
    h = (rg_lru(xc, ga_w[0], ga_b[0], gi_w[0], gi_b[0], lru_lam[0], False)
         + rg_lru(xc, ga_w[1], ga_b[1], gi_w[1], gi_b[1], lru_lam[1], True))
    return h * jax.nn.gelu(gate.astype(F32))


def peer(x, wq, sub_keys, eu, ev):
    B, S, D = x.shape
    xt = x.reshape(-1, PEER_CHUNK, D)

    def chunk(xc):
        xc = xc.astype(F32)
        q = (xc @ wq).reshape(PEER_CHUNK, PEER_HEADS, 2, PEER_HALF)
        s = jnp.einsum('chpd,pkd->chpk', q, sub_keys)
        s_top, i_top = lax.top_k(s, PEER_TOPK)
        cand = (s_top[:, :, 0, :, None] + s_top[:, :, 1, None, :]).reshape(
            PEER_CHUNK, PEER_HEADS, PEER_TOPK * PEER_TOPK)
        f_s, f_i = lax.top_k(cand, PEER_TOPK)
        e1 = jnp.take_along_axis(i_top[:, :, 0], f_i // PEER_TOPK, axis=-1)
        e2 = jnp.take_along_axis(i_top[:, :, 1], f_i % PEER_TOPK, axis=-1)
        idx = (e1 * N_KEYS + e2).reshape(PEER_CHUNK, PEER_HEADS * PEER_TOPK)
        g = jax.nn.softmax(f_s.astype(F32), axis=-1).reshape(PEER_CHUNK, PEER_HEADS * PEER_TOPK)
        u = jnp.take(eu, idx, axis=0).astype(F32)
        vv = jnp.take(ev, idx, axis=0).astype(F32)
        act = jax.nn.gelu(jnp.einsum('cnd,cd->cn', u, xc))
        return jnp.einsum('cn,cnd->cd', g * act, vv)

    return lax.map(chunk, xt).reshape(B, S, D)


def encoder_layer(x, l, w_in, lambda_qk, subln_g, conv_w, conv_b, gate_a_w, gate_a_b,
                  gate_i_w, gate_i_b, lru_lambda, w_out, ln1_g, ln1_b, peer_wq, peer_keys,
                  expert_u, expert_v, ln2_g, ln2_b):
    dtype = x.dtype
    B, S = x.shape[0], x.shape[1]
    lambda_init = 0.8 - 0.6 * math.exp(-0.3 * l)
    proj = x @ w_in[l]
    q, k, v, xr, gate = jnp.split(
        proj, [ATTN_WIDTH, 2 * ATTN_WIDTH, 3 * ATTN_WIDTH, 3 * ATTN_WIDTH + LRU_WIDTH], axis=-1)
    q = q.reshape(B, S, N_DIFF_HEADS, 2, QK_HEAD_DIM).astype(F32)
    k = k.reshape(B, S, N_DIFF_HEADS, 2, QK_HEAD_DIM).astype(F32)
    v = v.reshape(B, S, N_DIFF_HEADS, V_HEAD_DIM)
    cos, sin = rotary_tables(S)
    q = partial_rotary(q, cos, sin)
    k = partial_rotary(k, cos, sin)
    lp = lambda_qk[l].astype(F32)
    lam = jnp.exp(jnp.sum(lp[0] * lp[1])) - jnp.exp(jnp.sum(lp[2] * lp[3])) + lambda_init
    attn = diff_attention(q, k, v, lam, subln_g[l], lambda_init)
    rec = recurrent_mixer(xr, gate, conv_w[l], conv_b[l], gate_a_w[l], gate_a_b[l],
                          gate_i_w[l], gate_i_b[l], lru_lambda[l])
    mix = jnp.concatenate([attn, rec], axis=-1) @ w_out[l].astype(F32)
    x = layer_norm(DN_ALPHA * x.astype(F32) + mix, ln1_g[l], ln1_b[l]).astype(dtype)
    ffn = peer(x, peer_wq[l], peer_keys[l], expert_u[l], expert_v[l])
    x = layer_norm(DN_ALPHA * x.astype(F32) + ffn, ln2_g[l], ln2_b[l]).astype(dtype)
    return x


def trunk(x, w_in, lambda_qk, subln_g, conv_w, conv_b, gate_a_w, gate_a_b, gate_i_w,
          gate_i_b, lru_lambda, w_out, ln1_g, ln1_b, peer_wq, peer_keys, expert_u,
          expert_v, ln2_g, ln2_b):
    for l in range(DEPTH):
        x = encoder_layer(x, l, w_in, lambda_qk, subln_g, conv_w, conv_b, gate_a_w, gate_a_b,
                          gate_i_w, gate_i_b, lru_lambda, w_out, ln1_g, ln1_b, peer_wq,
                          peer_keys, expert_u, expert_v, ln2_g, ln2_b)
    return x


def setup_inputs(seed: int = 0) -> dict:
    key = jax.random.key(seed)
    ks = jax.random.split(key, 24)
    nrm = lambda k, shape, s: jax.random.normal(k, shape, F32) * s
    w_in = nrm(ks[2], (DEPTH, D_MODEL, IN_WIDTH), D_MODEL ** -0.5)
    w_in = w_in.at[..., 2 * ATTN_WIDTH:3 * ATTN_WIDTH].multiply(DN_BETA)
    u_lru = jax.random.uniform(ks[11], (DEPTH, 2, LRU_WIDTH), F32, 0.9, 0.999)
    return {
        "x_prompt": nrm(ks[0], (BATCH, SEQ, D_MODEL), 1.0),
        "x_sample": nrm(ks[1], (DEC_BATCH, DEC_SEQ, D_MODEL), 1.0),
        "w_in": w_in,
        "lambda_qk": nrm(ks[3], (DEPTH, 4, QK_HEAD_DIM), 0.1),
        "subln_g": 1.0 + nrm(ks[4], (DEPTH, V_HEAD_DIM), 0.02),
        "conv_w": nrm(ks[5], (DEPTH, CONV_WIDTH, LRU_WIDTH), CONV_WIDTH ** -0.5),
        "conv_b": nrm(ks[6], (DEPTH, LRU_WIDTH), 0.01),
        "gate_a_w": nrm(ks[7], (DEPTH, 2, LRU_BLOCKS, LRU_BLOCK_DIM, LRU_BLOCK_DIM), LRU_BLOCK_DIM ** -0.5),
        "gate_a_b": nrm(ks[8], (DEPTH, 2, LRU_WIDTH), 0.01),
        "gate_i_w": nrm(ks[9], (DEPTH, 2, LRU_BLOCKS, LRU_BLOCK_DIM, LRU_BLOCK_DIM), LRU_BLOCK_DIM ** -0.5),
        "gate_i_b": nrm(ks[10], (DEPTH, 2, LRU_WIDTH), 0.01),
        "lru_lambda": jnp.log(u_lru) - jnp.log1p(-u_lru),
        "w_out": nrm(ks[12], (DEPTH, ATTN_WIDTH + LRU_WIDTH, D_MODEL), DN_BETA * (ATTN_WIDTH + LRU_WIDTH) ** -0.5),
        "ln1_g": 1.0 + nrm(ks[13], (DEPTH, D_MODEL), 0.02),
        "ln1_b": nrm(ks[14], (DEPTH, D_MODEL), 0.01),
        "peer_wq": nrm(ks[15], (DEPTH, D_MODEL, PEER_HEADS * PEER_QUERY_DIM), D_MODEL ** -0.5),
        "peer_keys": nrm(ks[16], (DEPTH, 2, N_KEYS, PEER_HALF), PEER_HALF ** -0.5),
        "expert_u": nrm(ks[17], (DEPTH, N_EXPERTS, D_MODEL), D_MODEL ** -0.5),
        "expert_v": nrm(ks[18], (DEPTH, N_EXPERTS, D_MODEL), DN_BETA * PEER_HEADS ** -0.5),
        "ln2_g": 1.0 + nrm(ks[19], (DEPTH, D_MODEL), 0.02),
        "ln2_b": nrm(ks[20], (DEPTH, D_MODEL), 0.01),
    }


def reference(x_prompt, x_sample, w_in, lambda_qk, subln_g, conv_w, conv_b, gate_a_w, gate_a_b,
              gate_i_w, gate_i_b, lru_lambda, w_out, ln1_g, ln1_b, peer_wq, peer_keys,
              expert_u, expert_v, ln2_g, ln2_b):
    y_prompt = trunk(x_prompt, w_in, lambda_qk, subln_g, conv_w, conv_b, gate_a_w, gate_a_b,
                     gate_i_w, gate_i_b, lru_lambda, w_out, ln1_g, ln1_b, peer_wq, peer_keys,
                     expert_u, expert_v, ln2_g, ln2_b)
    y_sample = trunk(x_sample, w_in, lambda_qk, subln_g, conv_w, conv_b, gate_a_w, gate_a_b,
                     gate_i_w, gate_i_b, lru_lambda, w_out, ln1_g, ln1_b, peer_wq, peer_keys,
                     expert_u, expert_v, ln2_g, ln2_b)
    return (y_prompt, y_sample)
```

```python
import functools
import math

import jax
import jax.numpy as jnp
from jax import lax
from jax.experimental import pallas as pl
from jax.experimental.pallas import tpu as pltpu

F32 = jnp.float32
BF16 = jnp.bfloat16

D_MODEL = 1024
DEPTH = 2
ATTN_WIDTH = 512
LRU_WIDTH = 512
N_DIFF_HEADS = 4
QK_HEAD_DIM = 64
V_HEAD_DIM = 128
ROT_DIM = 16
ROPE_THETA = 500000.0
CONV_LEFT = 2
LRU_BLOCKS = 8
LRU_BLOCK_DIM = 64
LRU_C = 8.0
PEER_HEADS = 8
N_KEYS = 128
PEER_TOPK = 16
PEER_HALF = 128
IN_WIDTH = 3 * ATTN_WIDTH + 2 * LRU_WIDTH
DN_ALPHA = (2 * DEPTH) ** 0.25
LN_EPS = 1e-5
SUBLN_EPS = 1e-5

LANES = 128
SUBLANES = 8
NEG_INF = float("-inf")

_NT = (((1,), (1,)), ((), ()))


def _tile(n, want):
    t = min(n, want)
    assert n % t == 0, (n, t)
    return t


def _layer_norm(y, g, b):
    mu = jnp.mean(y, axis=-1, keepdims=True)
    yc = y - mu
    var = jnp.mean(yc * yc, axis=-1, keepdims=True)
    return yc * lax.rsqrt(var + LN_EPS) * g + b


def _gelu(x):
    return jax.nn.gelu(x, approximate=True)


def _proj_body(x_ref, w_ref, cos_ref, sa_ref, sb_ref, q_ref, k_ref, v_ref, xr_ref, gate_ref):
    xb = x_ref[0].astype(BF16)
    cos = cos_ref[...]
    sa = sa_ref[...]
    sb = sb_ref[...]

    def rotate(t):
        return t * cos + pltpu.roll(t, 8, 1) * sa + pltpu.roll(t, LANES - 8, 1) * sb

    def proj(c0, width):
        return jnp.dot(xb, w_ref[:, c0:c0 + width], preferred_element_type=F32)

    q = proj(0, ATTN_WIDTH)
    k = proj(ATTN_WIDTH, ATTN_WIDTH)
    for h in range(N_DIFF_HEADS):
        cs = slice(h * LANES, (h + 1) * LANES)
        q_ref[0, :, cs] = (rotate(q[:, cs]) * (QK_HEAD_DIM ** -0.5)).astype(BF16)
        k_ref[0, :, cs] = rotate(k[:, cs]).astype(BF16)
    v_ref[0] = proj(2 * ATTN_WIDTH, ATTN_WIDTH).astype(BF16)
    xr_ref[0] = proj(3 * ATTN_WIDTH, LRU_WIDTH)
    gate_ref[0] = proj(3 * ATTN_WIDTH + LRU_WIDTH, LRU_WIDTH)


def _proj(x, w_in, cos_t, sa_t, sb_t):
    B, S, D = x.shape
    tm = _tile(S, 512)
    row = lambda b, i: (b, i, 0)
    tab = lambda b, i: (i, 0)
    out_bf = jax.ShapeDtypeStruct((B, S, ATTN_WIDTH), BF16)
    out_f = jax.ShapeDtypeStruct((B, S, LRU_WIDTH), F32)
    return pl.pallas_call(
        _proj_body,
        grid=(B, S // tm),
        in_specs=[
            pl.BlockSpec((1, tm, D), row),
            pl.BlockSpec((D, IN_WIDTH), lambda b, i: (0, 0)),
            pl.BlockSpec((tm, LANES), tab),
            pl.BlockSpec((tm, LANES), tab),
            pl.BlockSpec((tm, LANES), tab),
        ],
        out_specs=[pl.BlockSpec((1, tm, ATTN_WIDTH), row)] * 5,
        out_shape=[out_bf, out_bf, out_bf, out_f, out_f],
        compiler_params=pltpu.CompilerParams(
            dimension_semantics=("arbitrary", "arbitrary")),
        name="proj",
    )(x, w_in, cos_t, sa_t, sb_t)


def _rotary_tables(S):
    half = ROT_DIM // 2
    inv = ROPE_THETA ** (-jnp.arange(0, ROT_DIM, 2, dtype=F32) / ROT_DIM)
    ang = jnp.arange(S, dtype=F32)[:, None] * inv[None, :]
    cos, sin = jnp.cos(ang), jnp.sin(ang)
    ones = jnp.ones((S, QK_HEAD_DIM - ROT_DIM), F32)
    zeros = jnp.zeros((S, QK_HEAD_DIM - ROT_DIM), F32)
    z8 = jnp.zeros((S, half), F32)
    cos_t = jnp.concatenate([cos, cos, ones], axis=-1)
    sa_t = jnp.concatenate([z8, sin, zeros], axis=-1)
    sb_t = jnp.concatenate([-sin, z8, zeros], axis=-1)
    rep = lambda t: jnp.concatenate([t, t], axis=-1)
    return rep(cos_t), rep(sa_t), rep(sb_t)


def _attn_body(lam_ref, g_ref, q_ref, k_ref, v_ref, o_ref, m_sc, l_sc, acc_sc, *, lambda_init):
    ik = pl.program_id(3)

    @pl.when(ik == 0)
    def _():
        m_sc[...] = jnp.full(m_sc.shape, NEG_INF, F32)
        l_sc[...] = jnp.zeros(l_sc.shape, F32)
        acc_sc[...] = jnp.zeros(acc_sc.shape, F32)

    q = q_ref[0]
    k = k_ref[0]
    v = v_ref[0]
    lane = lax.broadcasted_iota(jnp.int32, k.shape, 1)
    zero = jnp.zeros_like(k)
    for c in range(2):
        kc = jnp.where((lane < QK_HEAD_DIM) if c == 0 else (lane >= QK_HEAD_DIM), k, zero)
        s = lax.dot_general(q, kc, _NT, preferred_element_type=F32)
        m_prev = m_sc[c]
        m_new = jnp.maximum(m_prev, jnp.max(s, axis=1, keepdims=True))
        alpha = jnp.exp(m_prev - m_new)
        p = jnp.exp(s - m_new)
        l_sc[c] = alpha * l_sc[c] + jnp.sum(p, axis=1, keepdims=True)
        acc_sc[c] = alpha * acc_sc[c] + jnp.dot(p.astype(BF16), v, preferred_element_type=F32)
        m_sc[c] = m_new

    @pl.when(ik == pl.num_programs(3) - 1)
    def _():
        lp = lam_ref[...]
        d1 = jnp.sum(lp[0:1] * lp[1:2], axis=1, keepdims=True)
        d2 = jnp.sum(lp[2:3] * lp[3:4], axis=1, keepdims=True)
        lam = jnp.exp(d1) - jnp.exp(d2) + lambda_init
        o = acc_sc[0] / l_sc[0] - lam * (acc_sc[1] / l_sc[1])
        o = o * lax.rsqrt(jnp.mean(o * o, axis=1, keepdims=True) + SUBLN_EPS) * g_ref[...]
        o_ref[0] = (o * (1.0 - lambda_init)).astype(o_ref.dtype)


def _attention(q, k, v, lambda_qk, subln_g, lambda_init):
    B, S, _ = q.shape
    tq = _tile(S, 512)
    tk = _tile(S, 512)
    return pl.pallas_call(
        functools.partial(_attn_body, lambda_init=lambda_init),
        grid=(B, N_DIFF_HEADS, S // tq, S // tk),
        in_specs=[
            pl.BlockSpec((4, QK_HEAD_DIM), lambda b, h, i, j: (0, 0)),
            pl.BlockSpec((1, V_HEAD_DIM), lambda b, h, i, j: (0, 0)),
            pl.BlockSpec((1, tq, LANES), lambda b, h, i, j: (b, i, h)),
            pl.BlockSpec((1, tk, LANES), lambda b, h, i, j: (b, j, h)),
            pl.BlockSpec((1, tk, LANES), lambda b, h, i, j: (b, j, h)),
        ],
        out_specs=pl.BlockSpec((1, tq, LANES), lambda b, h, i, j: (b, i, h)),
        out_shape=jax.ShapeDtypeStruct((B, S, ATTN_WIDTH), BF16),
        scratch_shapes=[
            pltpu.VMEM((2, tq, 1), F32),
            pltpu.VMEM((2, tq, 1), F32),
            pltpu.VMEM((2, tq, V_HEAD_DIM), F32),
        ],
        compiler_params=pltpu.CompilerParams(
            dimension_semantics=("arbitrary", "arbitrary", "arbitrary", "arbitrary")),
        name="attn",
    )(lambda_qk, subln_g.reshape(1, V_HEAD_DIM), q, k, v)


def _lru_body(x_ref, prev_ref, next_ref, cw_ref, cb_ref, gw_ref, gb_ref, lam_ref, h_ref,
              a_sc, b_sc, carry_sc):
    d = pl.program_id(1)
    j = pl.program_id(2)
    n_t = pl.num_programs(2)
    tt = x_ref.shape[1]
    jj = jnp.where(d == 0, j, n_t - 1 - j)

    @pl.when(j == 0)
    def _():
        carry_sc[...] = jnp.zeros(carry_sc.shape, F32)

    x = x_ref[0]
    prev = prev_ref[0] * (jj > 0).astype(F32)
    nxt = next_ref[0] * (jj < n_t - 1).astype(F32)
    row = lax.broadcasted_iota(jnp.int32, x.shape, 0)
    cw = cw_ref[...]
    x_m1 = jnp.where(row == 0, prev[7:8], pltpu.roll(x, 1, 0))
    x_m2 = jnp.where(row == 0, prev[6:7], jnp.where(row == 1, prev[7:8], pltpu.roll(x, 2, 0)))
    x_p1 = jnp.where(row == tt - 1, nxt[0:1], pltpu.roll(x, tt - 1, 0))
    xc = x_m2 * cw[0:1] + x_m1 * cw[1:2] + x * cw[2:3] + x_p1 * cw[3:4] + cb_ref[...]

    gates = jnp.dot(xc.astype(BF16), gw_ref[0], preferred_element_type=F32) + gb_ref[0]
    r = jax.nn.sigmoid(gates[:, :LRU_WIDTH])
    i = jax.nn.sigmoid(gates[:, LRU_WIDTH:])
    lam = lam_ref[0]
    softplus_neg = jnp.maximum(-lam, 0.0) + jnp.log(1.0 + jnp.exp(-jnp.abs(lam)))
    log_a = -LRU_C * r * softplus_neg
    a = jnp.exp(log_a)
    b = jnp.sqrt(1.0 - a * a) * (i * xc)

    sub = row & (SUBLANES - 1)

    def local_scan(a, b, reverse):
        shift = 1
        while shift < SUBLANES:
            if reverse:
                a_s = pltpu.roll(a, tt - shift, 0)
                b_s = pltpu.roll(b, tt - shift, 0)
                valid = sub < SUBLANES - shift
            else:
                a_s = pltpu.roll(a, shift, 0)
                b_s = pltpu.roll(b, shift, 0)
                valid = sub >= shift
            b = a * jnp.where(valid, b_s, 0.0) + b
            a = a * jnp.where(valid, a_s, 1.0)
            shift *= 2
        return a, b

    n_groups = tt // SUBLANES

    def sweep(reverse):
        a_l, b_l = local_scan(a, b, reverse)
        a_sc[...] = a_l
        b_sc[...] = b_l

        def step(g, carry):
            gi = (n_groups - 1 - g) if reverse else g
            rs = pl.ds(pl.multiple_of(gi * SUBLANES, SUBLANES), SUBLANES)
            h = b_sc[rs, :] + a_sc[rs, :] * carry
            h_ref[0, 0, rs, :] = h
            return h[0:1] if reverse else h[SUBLANES - 1:SUBLANES]

        carry_sc[...] = lax.fori_loop(0, n_groups, step, carry_sc[...])

    @pl.when(d == 0)
    def _():
        sweep(False)

    @pl.when(d == 1)
    def _():
        sweep(True)


def _lru(xr, conv_w, conv_b, gate_w, gate_b, lru_lam):
    B, S, W = xr.shape
    tt = _tile(S, 1024)
    n_t = S // tt
    hb = tt // SUBLANES
    n_hb = S // SUBLANES

    def tblk(d, j):
        return jnp.where(d == 0, j, n_t - 1 - j)

    return pl.pallas_call(
        _lru_body,
        grid=(B, 2, n_t),
        in_specs=[
            pl.BlockSpec((1, tt, W), lambda b, d, j: (b, tblk(d, j), 0)),
            pl.BlockSpec((1, SUBLANES, W),
                         lambda b, d, j: (b, jnp.maximum(tblk(d, j) * hb - 1, 0), 0)),
            pl.BlockSpec((1, SUBLANES, W),
                         lambda b, d, j: (b, jnp.minimum((tblk(d, j) + 1) * hb, n_hb - 1), 0)),
            pl.BlockSpec((4, W), lambda b, d, j: (0, 0)),
            pl.BlockSpec((1, W), lambda b, d, j: (0, 0)),
            pl.BlockSpec((1, W, 2 * W), lambda b, d, j: (d, 0, 0)),
            pl.BlockSpec((1, 1, 2 * W), lambda b, d, j: (d, 0, 0)),
            pl.BlockSpec((1, 1, W), lambda b, d, j: (d, 0, 0)),
        ],
        out_specs=pl.BlockSpec((1, 1, tt, W), lambda b, d, j: (b, d, tblk(d, j), 0)),
        out_shape=jax.ShapeDtypeStruct((B, 2, S, W), F32),
        scratch_shapes=[
            pltpu.VMEM((tt, W), F32),
            pltpu.VMEM((tt, W), F32),
            pltpu.VMEM((1, W), F32),
        ],
        compiler_params=pltpu.CompilerParams(
            dimension_semantics=("arbitrary", "arbitrary", "arbitrary")),
        name="lru",
    )(xr, xr, xr, conv_w, conv_b.reshape(1, W), gate_w, gate_b, lru_lam)


def _block_diag(w):
    eye = jnp.eye(LRU_BLOCKS, dtype=w.dtype)
    return jnp.einsum('nde,nm->ndme', w, eye).reshape(LRU_WIDTH, LRU_WIDTH)


def _outproj_body(x_ref, attn_ref, hf_ref, hb_ref, gate_ref, wo_ref, g_ref, b_ref, o_ref):
    rec = (hf_ref[0, 0] + hb_ref[0, 0]) * _gelu(gate_ref[0])
    mix = jnp.dot(attn_ref[0], wo_ref[:ATTN_WIDTH], preferred_element_type=F32)
    mix += jnp.dot(rec.astype(BF16), wo_ref[ATTN_WIDTH:], preferred_element_type=F32)
    o_ref[0] = _layer_norm(DN_ALPHA * x_ref[0] + mix, g_ref[...], b_ref[...])


def _outproj(x, attn, h, gate, w_out, ln_g, ln_b):
    B, S, D = x.shape
    tm = _tile(S, 512)
    row = lambda b, i: (b, i, 0)
    const = lambda b, i: (0, 0)
    return pl.pallas_call(
        _outproj_body,
        grid=(B, S // tm),
        in_specs=[
            pl.BlockSpec((1, tm, D), row),
            pl.BlockSpec((1, tm, ATTN_WIDTH), row),
            pl.BlockSpec((1, 1, tm, LRU_WIDTH), lambda b, i: (b, 0, i, 0)),
            pl.BlockSpec((1, 1, tm, LRU_WIDTH), lambda b, i: (b, 1, i, 0)),
            pl.BlockSpec((1, tm, LRU_WIDTH), row),
            pl.BlockSpec((ATTN_WIDTH + LRU_WIDTH, D), const),
            pl.BlockSpec((1, D), const),
            pl.BlockSpec((1, D), const),
        ],
        out_specs=pl.BlockSpec((1, tm, D), row),
        out_shape=jax.ShapeDtypeStruct((B, S, D), F32),
        compiler_params=pltpu.CompilerParams(
            dimension_semantics=("arbitrary", "arbitrary")),
        name="outproj",
    )(x, attn, h, h, gate, w_out, ln_g.reshape(1, D), ln_b.reshape(1, D))


_CAND_PAIRS = [(i, j) for i in range(PEER_TOPK + 1) for j in range(PEER_TOPK + 1)
               if (i + 1) * (j + 1) <= PEER_TOPK + 1]
_CAND_ROWS = -(-len(_CAND_PAIRS) // SUBLANES) * SUBLANES


def _top_rows(s, n):
    rows = []
    cur = s
    for r in range(n):
        mx = jnp.max(cur, axis=0, keepdims=True)
        rows.append(mx)
        if r < n - 1:
            cur = jnp.where(cur == mx, NEG_INF, cur)
    return rows


def _route_body(x_ref, wq_ref, keys_ref, theta_ref, ga_ref, s2_ref, gb_ref, cand_sc):
    xb = x_ref[...].astype(BF16)
    qp = jnp.dot(xb, wq_ref[...], preferred_element_type=F32)
    cand_sc[...] = jnp.full(cand_sc.shape, NEG_INF, F32)
    for h in range(PEER_HEADS):
        s = []
        for p in range(2):
            c0 = (2 * h + p) * PEER_HALF
            blk = qp[:, c0:c0 + PEER_HALF].astype(BF16)
            s.append(lax.dot_general(keys_ref[p], blk, _NT, preferred_element_type=F32))
        s1, s2 = s
        a = _top_rows(s1, PEER_TOPK + 1)
        b = _top_rows(s2, PEER_TOPK + 1)
        for r, (i, j) in enumerate(_CAND_PAIRS):
            cand_sc[r:r + 1, :] = a[i] + b[j]
        cand = cand_sc[...]
        cur = cand
        for r in range(PEER_TOPK):
            mx = jnp.max(cur, axis=0, keepdims=True)
            cur = jnp.where(cur == mx, NEG_INF, cur)
        c16 = mx
        c17 = jnp.max(cur, axis=0, keepdims=True)
        thr = 0.5 * (c16 + c17)
        top = a[0] + b[0]
        z = jnp.sum(jnp.where(cand >= thr, jnp.exp(cand - top), 0.0), axis=0, keepdims=True)
        theta_ref[h] = thr - s1
        ga_ref[h] = jnp.exp(s1 - a[0]) / z
        s2_ref[h] = s2
        gb_ref[h] = jnp.exp(s2 - b[0])


def _route(x1, wq, keys):
    T, D = x1.shape
    tm = _tile(T, 256)
    out = jax.ShapeDtypeStruct((PEER_HEADS, N_KEYS, T), F32)
    ospec = pl.BlockSpec((PEER_HEADS, N_KEYS, tm), lambda i: (0, 0, i))
    return pl.pallas_call(
        _route_body,
        grid=(T // tm,),
        in_specs=[
            pl.BlockSpec((tm, D), lambda i: (i, 0)),
            pl.BlockSpec((D, 2 * PEER_HEADS * PEER_HALF), lambda i: (0, 0)),
            pl.BlockSpec((2, N_KEYS, PEER_HALF), lambda i: (0, 0, 0)),
        ],
        out_specs=[ospec] * 4,
        out_shape=[out] * 4,
        scratch_shapes=[pltpu.VMEM((_CAND_ROWS, tm), F32)],
        compiler_params=pltpu.CompilerParams(dimension_semantics=("arbitrary",)),
        name="route",
    )(x1, wq, keys)


PEER_GROUPS_PER_STEP = 8
PEER_SLAB = 16


def _peer_body(x_ref, theta_ref, ga_ref, s2_ref, gb_ref, eu_ref, evt_ref, g_ref, b_ref, o_ref,
               xb_sc, act_sc, wg_sc, acc_sc):
    j = pl.program_id(1)
    tc = x_ref.shape[0]

    @pl.when(j == 0)
    def _():
        xb_sc[...] = x_ref[...].astype(BF16)
        acc_sc[...] = jnp.zeros(acc_sc.shape, F32)

    act_sc[...] = lax.dot_general(eu_ref[...], xb_sc[...], _NT, preferred_element_type=F32)
    for g in range(PEER_GROUPS_PER_STEP):
        for lc in range(tc // LANES):
            ls = slice(lc * LANES, (lc + 1) * LANES)
            th = [jnp.broadcast_to(theta_ref[h, g:g + 1, ls], (PEER_SLAB, LANES))
                  for h in range(PEER_HEADS)]
            ga = [jnp.broadcast_to(ga_ref[h, g:g + 1, ls], (PEER_SLAB, LANES))
                  for h in range(PEER_HEADS)]
            for sl in range(N_KEYS // PEER_SLAB):
                rs = slice(sl * PEER_SLAB, (sl + 1) * PEER_SLAB)
                es = slice(g * N_KEYS + sl * PEER_SLAB, g * N_KEYS + (sl + 1) * PEER_SLAB)
                w = jnp.zeros((PEER_SLAB, LANES), F32)
                for h in range(PEER_HEADS):
                    w += jnp.where(s2_ref[h, rs, ls] >= th[h], ga[h], 0.0) * gb_ref[h, rs, ls]
                wg_sc[es, ls] = (w * _gelu(act_sc[es, ls])).astype(BF16)
    acc_sc[...] += jnp.dot(evt_ref[...], wg_sc[...], preferred_element_type=F32)

    @pl.when(j == pl.num_programs(1) - 1)
    def _():
        y = DN_ALPHA * x_ref[...] + acc_sc[...].T
        o_ref[...] = _layer_norm(y, g_ref[...], b_ref[...])


def _peer(x1, theta, ga, s2, gb, eu, evt, ln_g, ln_b):
    T, D = x1.shape
    n_exp = eu.shape[0]
    tc = _tile(T, 512)
    gs = PEER_GROUPS_PER_STEP
    ne = gs * N_KEYS
    per_group = pl.BlockSpec((PEER_HEADS, gs, tc), lambda i, j: (0, j, i))
    per_key = pl.BlockSpec((PEER_HEADS, N_KEYS, tc), lambda i, j: (0, 0, i))
    const = lambda i, j: (0, 0)
    return pl.pallas_call(
        _peer_body,
        grid=(T // tc, n_exp // ne),
        in_specs=[
            pl.BlockSpec((tc, D), lambda i, j: (i, 0)),
            per_group, per_group, per_key, per_key,
            pl.BlockSpec((ne, D), lambda i, j: (j, 0)),
            pl.BlockSpec((D, ne), lambda i, j: (0, j)),
            pl.BlockSpec((1, D), const),
            pl.BlockSpec((1, D), const),
        ],
        out_specs=pl.BlockSpec((tc, D), lambda i, j: (i, 0)),
        out_shape=jax.ShapeDtypeStruct((T, D), F32),
        scratch_shapes=[
            pltpu.VMEM((tc, D), BF16),
            pltpu.VMEM((ne, tc), F32),
            pltpu.VMEM((ne, tc), BF16),
            pltpu.VMEM((D, tc), F32),
        ],
        compiler_params=pltpu.CompilerParams(
            dimension_semantics=("arbitrary", "arbitrary")),
        name="peer",
    )(x1, theta, ga, s2, gb, eu, evt, ln_g.reshape(1, D), ln_b.reshape(1, D))


def _trunk(x, w_in, lambda_qk, subln_g, conv_w, conv_b, gate_a_w, gate_a_b, gate_i_w,
           gate_i_b, lru_lambda, w_out, ln1_g, ln1_b, peer_wq, peer_keys, expert_u,
           expert_v, ln2_g, ln2_b):
    B, S, D = x.shape
    cos_t, sa_t, sb_t = _rotary_tables(S)
    for l in range(DEPTH):
        lambda_init = 0.8 - 0.6 * math.exp(-0.3 * l)
        q, k, v, xr, gate = _proj(x, w_in[l].astype(BF16), cos_t, sa_t, sb_t)
        attn = _attention(q, k, v, lambda_qk[l], subln_g[l], lambda_init)
        gate_w = jnp.stack([
            jnp.concatenate([_block_diag(gate_a_w[l, d]), _block_diag(gate_i_w[l, d])], axis=1)
            for d in range(2)]).astype(BF16)
        gate_b = jnp.concatenate([gate_a_b[l], gate_i_b[l]], axis=-1)[:, None, :]
        h = _lru(xr, conv_w[l], conv_b[l], gate_w, gate_b, lru_lambda[l][:, None, :])
        x1 = _outproj(x, attn, h, gate, w_out[l].astype(BF16), ln1_g[l], ln1_b[l])
        x1 = x1.reshape(B * S, D)
        theta, ga, s2, gb = _route(x1, peer_wq[l].astype(BF16), peer_keys[l].astype(BF16))
        x = _peer(x1, theta, ga, s2, gb, expert_u[l].astype(BF16),
                  expert_v[l].astype(BF16).T, ln2_g[l], ln2_b[l]).reshape(B, S, D)
    return x


def kernel(x_prompt, x_sample, w_in, lambda_qk, subln_g, conv_w, conv_b, gate_a_w, gate_a_b,
           gate_i_w, gate_i_b, lru_lambda, w_out, ln1_g, ln1_b, peer_wq, peer_keys, expert_u,
           expert_v, ln2_g, ln2_b):
    assert x_prompt.shape[1:] == x_sample.shape[1:]
    nb = x_prompt.shape[0]
    x = jnp.concatenate([x_prompt, x_sample], axis=0)
    y = _trunk(x, w_in, lambda_qk, subln_g, conv_w, conv_b, gate_a_w, gate_a_b, gate_i_w,
               gate_i_b, lru_lambda, w_out, ln1_g, ln1_b, peer_wq, peer_keys, expert_u,
               expert_v, ln2_g, ln2_b)
    return (y[:nb], y[nb:])
```

```python
import functools
import math

import jax
import jax.numpy as jnp
from jax import lax
from jax.experimental import pallas as pl
from jax.experimental.pallas import tpu as pltpu

F32 = jnp.float32
BF16 = jnp.bfloat16

D_MODEL = 1024
DEPTH = 2
ATTN_WIDTH = 512
LRU_WIDTH = 512
N_DIFF_HEADS = 4
QK_HEAD_DIM = 64
V_HEAD_DIM = 128
ROT_DIM = 16
ROPE_THETA = 500000.0
CONV_LEFT = 2
LRU_BLOCKS = 8
LRU_BLOCK_DIM = 64
LRU_C = 8.0
PEER_HEADS = 8
N_KEYS = 128
PEER_TOPK = 16
PEER_HALF = 128
IN_WIDTH = 3 * ATTN_WIDTH + 2 * LRU_WIDTH
DN_ALPHA = (2 * DEPTH) ** 0.25
LN_EPS = 1e-5
SUBLN_EPS = 1e-5

LANES = 128
SUBLANES = 8
NEG_INF = float("-inf")
LOG2_E = math.log2(math.e)

_NT = (((1,), (1,)), ((), ()))


def _tile(n, want):
    t = min(n, want)
    assert n % t == 0, (n, t)
    return t


def _layer_norm(y, g, b):
    mu = jnp.mean(y, axis=-1, keepdims=True)
    yc = y - mu
    var = jnp.mean(yc * yc, axis=-1, keepdims=True)
    return yc * lax.rsqrt(var + LN_EPS) * g + b


def _gelu(x):
    c = math.sqrt(2.0 / math.pi)
    hx = 0.5 * x
    return hx + hx * jnp.tanh(x * (c + (c * 0.044715) * (x * x)))


def _proj_body(x_ref, w_ref, cos_ref, sa_ref, sb_ref, q_ref, k_ref, v_ref, xr_ref, gate_ref):
    xb = x_ref[0].astype(BF16)
    cos = cos_ref[...]
    sa = sa_ref[...]
    sb = sb_ref[...]

    def rotate(t):
        return t * cos + pltpu.roll(t, 8, 1) * sa + pltpu.roll(t, LANES - 8, 1) * sb

    def proj(c0, width):
        return jnp.dot(xb, w_ref[:, c0:c0 + width], preferred_element_type=F32)

    q = proj(0, ATTN_WIDTH)
    k = proj(ATTN_WIDTH, ATTN_WIDTH)
    for h in range(N_DIFF_HEADS):
        cs = slice(h * LANES, (h + 1) * LANES)
        q_ref[0, :, cs] = (rotate(q[:, cs]) * (QK_HEAD_DIM ** -0.5 * LOG2_E)).astype(BF16)
        k_ref[0, :, cs] = rotate(k[:, cs]).astype(BF16)
    v_ref[0] = proj(2 * ATTN_WIDTH, ATTN_WIDTH).astype(BF16)
    xr_ref[0] = proj(3 * ATTN_WIDTH, LRU_WIDTH)
    gate_ref[0] = proj(3 * ATTN_WIDTH + LRU_WIDTH, LRU_WIDTH)


def _proj(x, w_in, cos_t, sa_t, sb_t):
    B, S, D = x.shape
    tm = _tile(S, 512)
    row = lambda b, i: (b, i, 0)
    tab = lambda b, i: (i, 0)
    out_bf = jax.ShapeDtypeStruct((B, S, ATTN_WIDTH), BF16)
    out_f = jax.ShapeDtypeStruct((B, S, LRU_WIDTH), F32)
    return pl.pallas_call(
        _proj_body,
        grid=(B, S // tm),
        in_specs=[
            pl.BlockSpec((1, tm, D), row),
            pl.BlockSpec((D, IN_WIDTH), lambda b, i: (0, 0)),
            pl.BlockSpec((tm, LANES), tab),
            pl.BlockSpec((tm, LANES), tab),
            pl.BlockSpec((tm, LANES), tab),
        ],
        out_specs=[pl.BlockSpec((1, tm, ATTN_WIDTH), row)] * 5,
        out_shape=[out_bf, out_bf, out_bf, out_f, out_f],
        compiler_params=pltpu.CompilerParams(
            dimension_semantics=("arbitrary", "arbitrary")),
        name="proj",
    )(x, w_in, cos_t, sa_t, sb_t)


def _rotary_tables(S):
    half = ROT_DIM // 2
    inv = ROPE_THETA ** (-jnp.arange(0, ROT_DIM, 2, dtype=F32) / ROT_DIM)
    ang = jnp.arange(S, dtype=F32)[:, None] * inv[None, :]
    cos, sin = jnp.cos(ang), jnp.sin(ang)
    ones = jnp.ones((S, QK_HEAD_DIM - ROT_DIM), F32)
    zeros = jnp.zeros((S, QK_HEAD_DIM - ROT_DIM), F32)
    z8 = jnp.zeros((S, half), F32)
    cos_t = jnp.concatenate([cos, cos, ones], axis=-1)
    sa_t = jnp.concatenate([z8, sin, zeros], axis=-1)
    sb_t = jnp.concatenate([-sin, z8, zeros], axis=-1)
    rep = lambda t: jnp.concatenate([t, t], axis=-1)
    return rep(cos_t), rep(sa_t), rep(sb_t)


def _attn_body(lam_ref, g_ref, q_ref, k_ref, v_ref, o_ref, km_sc, va_sc, m_sc, acc_sc, *,
               lambda_init, tk):
    seq = k_ref.shape[1]

    @pl.when(pl.program_id(2) == 0)
    def _():
        k = k_ref[0]
        lane = lax.broadcasted_iota(jnp.int32, k.shape, 1)
        zero = jnp.zeros_like(k)
        km_sc[0] = jnp.where(lane < QK_HEAD_DIM, k, zero)
        km_sc[1] = jnp.where(lane >= QK_HEAD_DIM, k, zero)
        va_sc[:, :V_HEAD_DIM] = v_ref[0]
        va_sc[:, V_HEAD_DIM:] = jnp.ones((seq, V_HEAD_DIM), BF16)

    m_sc[...] = jnp.full(m_sc.shape, NEG_INF, F32)
    acc_sc[...] = jnp.zeros(acc_sc.shape, F32)
    q = q_ref[0]

    def chunk(j, carry):
        ks = pl.ds(pl.multiple_of(j * tk, tk), tk)
        for c in range(2):
            s = lax.dot_general(q, km_sc[c, ks, :], _NT, preferred_element_type=F32)
            m_prev = m_sc[c]
            m_new = jnp.maximum(m_prev, jnp.max(s, axis=1, keepdims=True))
            alpha = jnp.exp2(m_prev - m_new)
            p = jnp.exp2(s - jnp.tile(m_new, (1, tk // LANES)))
            pv = jnp.dot(p.astype(BF16), va_sc[ks, :], preferred_element_type=F32)
            acc_sc[c] = jnp.tile(alpha, (1, 2)) * acc_sc[c] + pv
            m_sc[c] = m_new
        return carry

    lax.fori_loop(0, seq // tk, chunk, 0)

    lp = lam_ref[...]
    d1 = jnp.sum(lp[0:1] * lp[1:2], axis=1, keepdims=True)
    d2 = jnp.sum(lp[2:3] * lp[3:4], axis=1, keepdims=True)
    lam = jnp.exp(d1) - jnp.exp(d2) + lambda_init
    acc1 = acc_sc[0]
    acc2 = acc_sc[1]
    o = (acc1[:, :V_HEAD_DIM] / acc1[:, V_HEAD_DIM:]
         - lam * (acc2[:, :V_HEAD_DIM] / acc2[:, V_HEAD_DIM:]))
    o = o * lax.rsqrt(jnp.mean(o * o, axis=1, keepdims=True) + SUBLN_EPS) * g_ref[...]
    o_ref[0] = (o * (1.0 - lambda_init)).astype(o_ref.dtype)


def _attention(q, k, v, lambda_qk, subln_g, lambda_init):
    B, S, _ = q.shape
    tq = _tile(S, 512)
    tk = _tile(S, 1024)
    return pl.pallas_call(
        functools.partial(_attn_body, lambda_init=lambda_init, tk=tk),
        grid=(B, N_DIFF_HEADS, S // tq),
        in_specs=[
            pl.BlockSpec((4, QK_HEAD_DIM), lambda b, h, i: (0, 0)),
            pl.BlockSpec((1, V_HEAD_DIM), lambda b, h, i: (0, 0)),
            pl.BlockSpec((1, tq, LANES), lambda b, h, i: (b, i, h)),
            pl.BlockSpec((1, S, LANES), lambda b, h, i: (b, 0, h)),
            pl.BlockSpec((1, S, LANES), lambda b, h, i: (b, 0, h)),
        ],
        out_specs=pl.BlockSpec((1, tq, LANES), lambda b, h, i: (b, i, h)),
        out_shape=jax.ShapeDtypeStruct((B, S, ATTN_WIDTH), BF16),
        scratch_shapes=[
            pltpu.VMEM((2, S, LANES), BF16),
            pltpu.VMEM((S, 2 * V_HEAD_DIM), BF16),
            pltpu.VMEM((2, tq, LANES), F32),
            pltpu.VMEM((2, tq, 2 * V_HEAD_DIM), F32),
        ],
        compiler_params=pltpu.CompilerParams(
            dimension_semantics=("arbitrary", "arbitrary", "arbitrary")),
        name="attn",
    )(lambda_qk, subln_g.reshape(1, V_HEAD_DIM), q, k, v)


def _lru_body(x_ref, prev_ref, next_ref, cw_ref, cb_ref, gw_ref, gb_ref, lam_ref, h_ref,
              a_sc, b_sc, carry_sc):
    d = pl.program_id(1)
    j = pl.program_id(2)
    n_t = pl.num_programs(2)
    tt = x_ref.shape[1]
    jj = jnp.where(d == 0, j, n_t - 1 - j)

    @pl.when(j == 0)
    def _():
        carry_sc[...] = jnp.zeros(carry_sc.shape, F32)

    x = x_ref[0]
    prev = prev_ref[0] * (jj > 0).astype(F32)
    nxt = next_ref[0] * (jj < n_t - 1).astype(F32)
    row = lax.broadcasted_iota(jnp.int32, x.shape, 0)
    cw = cw_ref[...]
    x_m1 = jnp.where(row == 0, prev[7:8], pltpu.roll(x, 1, 0))
    x_m2 = jnp.where(row == 0, prev[6:7], jnp.where(row == 1, prev[7:8], pltpu.roll(x, 2, 0)))
    x_p1 = jnp.where(row == tt - 1, nxt[0:1], pltpu.roll(x, tt - 1, 0))
    xc = x_m2 * cw[0:1] + x_m1 * cw[1:2] + x * cw[2:3] + x_p1 * cw[3:4] + cb_ref[...]

    gates = jnp.dot(xc.astype(BF16), gw_ref[0], preferred_element_type=F32) + gb_ref[0]
    r = jax.nn.sigmoid(gates[:, :LRU_WIDTH])
    i = jax.nn.sigmoid(gates[:, LRU_WIDTH:])
    lam = lam_ref[0]
    softplus_neg = jnp.maximum(-lam, 0.0) + jnp.log(1.0 + jnp.exp(-jnp.abs(lam)))
    log_a = -LRU_C * r * softplus_neg
    a = jnp.exp(log_a)
    b = jnp.sqrt(1.0 - a * a) * (i * xc)

    sub = row & (SUBLANES - 1)

    def local_scan(a, b, reverse):
        shift = 1
        while shift < SUBLANES:
            if reverse:
                a_s = pltpu.roll(a, tt - shift, 0)
                b_s = pltpu.roll(b, tt - shift, 0)
                valid = sub < SUBLANES - shift
            else:
                a_s = pltpu.roll(a, shift, 0)
                b_s = pltpu.roll(b, shift, 0)
                valid = sub >= shift
            b = a * jnp.where(valid, b_s, 0.0) + b
            a = a * jnp.where(valid, a_s, 1.0)
            shift *= 2
        return a, b

    n_groups = tt // SUBLANES

    def sweep(reverse):
        a_l, b_l = local_scan(a, b, reverse)
        a_sc[...] = a_l
        b_sc[...] = b_l

        def step(g, carry):
            gi = (n_groups - 1 - g) if reverse else g
            rs = pl.ds(pl.multiple_of(gi * SUBLANES, SUBLANES), SUBLANES)
            h = b_sc[rs, :] + a_sc[rs, :] * carry
            h_ref[0, 0, rs, :] = h
            return h[0:1] if reverse else h[SUBLANES - 1:SUBLANES]

        carry_sc[...] = lax.fori_loop(0, n_groups, step, carry_sc[...])

    @pl.when(d == 0)
    def _():
        sweep(False)

    @pl.when(d == 1)
    def _():
        sweep(True)


def _lru(xr, conv_w, conv_b, gate_w, gate_b, lru_lam):
    B, S, W = xr.shape
    tt = _tile(S, 1024)
    n_t = S // tt
    hb = tt // SUBLANES
    n_hb = S // SUBLANES

    def tblk(d, j):
        return jnp.where(d == 0, j, n_t - 1 - j)

    return pl.pallas_call(
        _lru_body,
        grid=(B, 2, n_t),
        in_specs=[
            pl.BlockSpec((1, tt, W), lambda b, d, j: (b, tblk(d, j), 0)),
            pl.BlockSpec((1, SUBLANES, W),
                         lambda b, d, j: (b, jnp.maximum(tblk(d, j) * hb - 1, 0), 0)),
            pl.BlockSpec((1, SUBLANES, W),
                         lambda b, d, j: (b, jnp.minimum((tblk(d, j) + 1) * hb, n_hb - 1), 0)),
            pl.BlockSpec((4, W), lambda b, d, j: (0, 0)),
            pl.BlockSpec((1, W), lambda b, d, j: (0, 0)),
            pl.BlockSpec((1, W, 2 * W), lambda b, d, j: (d, 0, 0)),
            pl.BlockSpec((1, 1, 2 * W), lambda b, d, j: (d, 0, 0)),
            pl.BlockSpec((1, 1, W), lambda b, d, j: (d, 0, 0)),
        ],
        out_specs=pl.BlockSpec((1, 1, tt, W), lambda b, d, j: (b, d, tblk(d, j), 0)),
        out_shape=jax.ShapeDtypeStruct((B, 2, S, W), F32),
        scratch_shapes=[
            pltpu.VMEM((tt, W), F32),
            pltpu.VMEM((tt, W), F32),
            pltpu.VMEM((1, W), F32),
        ],
        compiler_params=pltpu.CompilerParams(
            dimension_semantics=("arbitrary", "arbitrary", "arbitrary")),
        name="lru",
    )(xr, xr, xr, conv_w, conv_b.reshape(1, W), gate_w, gate_b, lru_lam)


def _block_diag(w):
    eye = jnp.eye(LRU_BLOCKS, dtype=w.dtype)
    return jnp.einsum('nde,nm->ndme', w, eye).reshape(LRU_WIDTH, LRU_WIDTH)


def _outproj_body(x_ref, attn_ref, hf_ref, hb_ref, gate_ref, wo_ref, g_ref, b_ref, o_ref):
    rec = (hf_ref[0, 0] + hb_ref[0, 0]) * _gelu(gate_ref[0])
    mix = jnp.dot(attn_ref[0], wo_ref[:ATTN_WIDTH], preferred_element_type=F32)
    mix += jnp.dot(rec.astype(BF16), wo_ref[ATTN_WIDTH:], preferred_element_type=F32)
    o_ref[0] = _layer_norm(DN_ALPHA * x_ref[0] + mix, g_ref[...], b_ref[...])


def _outproj(x, attn, h, gate, w_out, ln_g, ln_b):
    B, S, D = x.shape
    tm = _tile(S, 512)
    row = lambda b, i: (b, i, 0)
    const = lambda b, i: (0, 0)
    return pl.pallas_call(
        _outproj_body,
        grid=(B, S // tm),
        in_specs=[
            pl.BlockSpec((1, tm, D), row),
            pl.BlockSpec((1, tm, ATTN_WIDTH), row),
            pl.BlockSpec((1, 1, tm, LRU_WIDTH), lambda b, i: (b, 0, i, 0)),
            pl.BlockSpec((1, 1, tm, LRU_WIDTH), lambda b, i: (b, 1, i, 0)),
            pl.BlockSpec((1, tm, LRU_WIDTH), row),
            pl.BlockSpec((ATTN_WIDTH + LRU_WIDTH, D), const),
            pl.BlockSpec((1, D), const),
            pl.BlockSpec((1, D), const),
        ],
        out_specs=pl.BlockSpec((1, tm, D), row),
        out_shape=jax.ShapeDtypeStruct((B, S, D), F32),
        compiler_params=pltpu.CompilerParams(
            dimension_semantics=("arbitrary", "arbitrary")),
        name="outproj",
    )(x, attn, h, h, gate, w_out, ln_g.reshape(1, D), ln_b.reshape(1, D))


_CAND_PAIRS = [(i, j) for i in range(PEER_TOPK + 1) for j in range(PEER_TOPK + 1)
               if (i + 1) * (j + 1) <= PEER_TOPK + 1]
_CAND_ROWS = -(-len(_CAND_PAIRS) // SUBLANES) * SUBLANES


def _top_rows(s, n, want_rank):
    rows = []
    cur = s
    rank = jnp.full(s.shape, float(n), F32) if want_rank else None
    for r in range(n):
        mx = jnp.max(cur, axis=0, keepdims=True)
        rows.append(mx)
        hit = cur == mx
        if want_rank:
            rank = jnp.where(hit, float(r), rank)
        if r < n - 1:
            cur = jnp.where(hit, NEG_INF, cur)
    return rows, rank


def _dup_bf16(x):
    u = pltpu.bitcast(x.astype(BF16).astype(F32), jnp.uint32)
    return u | lax.shift_right_logical(u, jnp.uint32(16))


def _route_body(x_ref, wq_ref, keys_ref, cnt_ref, ga_ref, rank_ref, gb_ref, cand_sc):
    xb = x_ref[...].astype(BF16)
    qp = jnp.dot(xb, wq_ref[...], preferred_element_type=F32)
    cand_sc[...] = jnp.full(cand_sc.shape, NEG_INF, F32)
    n = PEER_TOPK + 1
    for h in range(PEER_HEADS):
        s = []
        for p in range(2):
            c0 = (2 * h + p) * PEER_HALF
            blk = qp[:, c0:c0 + PEER_HALF].astype(BF16)
            s.append(lax.dot_general(keys_ref[p], blk, _NT, preferred_element_type=F32))
        s1, s2 = s
        a, _ = _top_rows(s1, n, False)
        b, rank2 = _top_rows(s2, n, True)
        for r, (i, j) in enumerate(_CAND_PAIRS):
            cand_sc[r:r + 1, :] = a[i] + b[j]
        cand = cand_sc[...]
        cur = cand
        for r in range(PEER_TOPK):
            mx = jnp.max(cur, axis=0, keepdims=True)
            cur = jnp.where(cur == mx, NEG_INF, cur)
        thr = 0.5 * (mx + jnp.max(cur, axis=0, keepdims=True))
        z = jnp.sum(jnp.where(cand >= thr, jnp.exp(cand - (a[0] + b[0])), 0.0),
                    axis=0, keepdims=True)
        cnt = jnp.zeros(s1.shape, F32)
        for j in range(n):
            cnt += jnp.where(s1 + b[j] >= thr, 1.0, 0.0)
        cnt_ref[h] = _dup_bf16(cnt)
        ga_ref[h] = _dup_bf16(jnp.exp(s1 - a[0]) / z)
        rank_ref[h] = pltpu.bitcast(rank2.astype(BF16), jnp.uint32)
        gb_ref[h] = pltpu.bitcast(jnp.exp(s2 - b[0]).astype(BF16), jnp.uint32)


def _route(x1, wq, keys):
    T, D = x1.shape
    tm = _tile(T, 256)
    shape = (PEER_HEADS, N_KEYS, T)
    packed = (PEER_HEADS, N_KEYS // 2, T)
    ospec = pl.BlockSpec((PEER_HEADS, N_KEYS, tm), lambda i: (0, 0, i))
    pspec = pl.BlockSpec((PEER_HEADS, N_KEYS // 2, tm), lambda i: (0, 0, i))
    return pl.pallas_call(
        _route_body,
        grid=(T // tm,),
        in_specs=[
            pl.BlockSpec((tm, D), lambda i: (i, 0)),
            pl.BlockSpec((D, 2 * PEER_HEADS * PEER_HALF), lambda i: (0, 0)),
            pl.BlockSpec((2, N_KEYS, PEER_HALF), lambda i: (0, 0, 0)),
        ],
        out_specs=[ospec, ospec, pspec, pspec],
        out_shape=[jax.ShapeDtypeStruct(shape, jnp.uint32), jax.ShapeDtypeStruct(shape, jnp.uint32),
                   jax.ShapeDtypeStruct(packed, jnp.uint32), jax.ShapeDtypeStruct(packed, jnp.uint32)],
        scratch_shapes=[pltpu.VMEM((_CAND_ROWS, tm), F32)],
        compiler_params=pltpu.CompilerParams(dimension_semantics=("arbitrary",)),
        name="route",
    )(x1, wq, keys)


PEER_GROUPS_PER_STEP = 8
PEER_SLAB = 16
PEER_SUB = 256


def _peer_body(x_ref, cnt_ref, ga_ref, rank_ref, gb_ref, eu_ref, evt_ref, g_ref, b_ref, o_ref,
               xb_sc, wg_sc, acc_sc):
    j = pl.program_id(1)
    tc = x_ref.shape[0]

    @pl.when(j == 0)
    def _():
        xb_sc[...] = x_ref[...].astype(BF16)
        acc_sc[...] = jnp.zeros(acc_sc.shape, F32)

    xb = xb_sc[...]

    def rows16(ref, h, g, ls):
        return pltpu.bitcast(jnp.broadcast_to(ref[h, g:g + 1, ls], (SUBLANES, LANES)), BF16)

    total = None
    for sb in range(PEER_GROUPS_PER_STEP * N_KEYS // PEER_SUB):
        sub = slice(sb * PEER_SUB, (sb + 1) * PEER_SUB)
        act = lax.dot_general(eu_ref[sub, :], xb, _NT, preferred_element_type=F32)
        for gg in range(PEER_SUB // N_KEYS):
            g = sb * (PEER_SUB // N_KEYS) + gg
            for lc in range(tc // LANES):
                ls = slice(lc * LANES, (lc + 1) * LANES)
                cnt = [rows16(cnt_ref, h, g, ls) for h in range(PEER_HEADS)]
                ga = [rows16(ga_ref, h, g, ls) for h in range(PEER_HEADS)]
                for sl in range(N_KEYS // PEER_SLAB):
                    rs = slice(sl * SUBLANES, (sl + 1) * SUBLANES)
                    a0 = gg * N_KEYS + sl * PEER_SLAB
                    w = None
                    for h in range(PEER_HEADS):
                        rank = pltpu.bitcast(rank_ref[h, rs, ls], BF16)
                        gb = pltpu.bitcast(gb_ref[h, rs, ls], BF16)
                        t = jnp.where(rank < cnt[h], ga[h], 0.0) * gb
                        w = t if w is None else w + t
                    gl = _gelu(act[a0:a0 + PEER_SLAB, ls]).astype(BF16)
                    wg_sc[sb * PEER_SUB + a0:sb * PEER_SUB + a0 + PEER_SLAB, ls] = w * gl
        part = jnp.dot(evt_ref[:, sub], wg_sc[sub, :], preferred_element_type=F32)
        total = part if total is None else total + part
    acc_sc[...] += total

    @pl.when(j == pl.num_programs(1) - 1)
    def _():
        y = DN_ALPHA * x_ref[...] + acc_sc[...].T
        o_ref[...] = _layer_norm(y, g_ref[...], b_ref[...])


def _peer(x1, cnt, ga, rank2, gb, eu, evt, ln_g, ln_b):
    T, D = x1.shape
    n_exp = eu.shape[0]
    tc = _tile(T, 512)
    gs = PEER_GROUPS_PER_STEP
    ne = gs * N_KEYS
    per_group = pl.BlockSpec((PEER_HEADS, gs, tc), lambda i, j: (0, j, i))
    per_key = pl.BlockSpec((PEER_HEADS, N_KEYS // 2, tc), lambda i, j: (0, 0, i))
    const = lambda i, j: (0, 0)
    return pl.pallas_call(
        _peer_body,
        grid=(T // tc, n_exp // ne),
        in_specs=[
            pl.BlockSpec((tc, D), lambda i, j: (i, 0)),
            per_group, per_group, per_key, per_key,
            pl.BlockSpec((ne, D), lambda i, j: (j, 0)),
            pl.BlockSpec((D, ne), lambda i, j: (0, j)),
            pl.BlockSpec((1, D), const),
            pl.BlockSpec((1, D), const),
        ],
        out_specs=pl.BlockSpec((tc, D), lambda i, j: (i, 0)),
        out_shape=jax.ShapeDtypeStruct((T, D), F32),
        scratch_shapes=[
            pltpu.VMEM((tc, D), BF16),
            pltpu.VMEM((ne, tc), BF16),
            pltpu.VMEM((D, tc), F32),
        ],
        compiler_params=pltpu.CompilerParams(
            dimension_semantics=("arbitrary", "arbitrary")),
        name="peer",
    )(x1, cnt, ga, rank2, gb, eu, evt, ln_g.reshape(1, D), ln_b.reshape(1, D))


def _trunk(x, w_in, lambda_qk, subln_g, conv_w, conv_b, gate_a_w, gate_a_b, gate_i_w,
           gate_i_b, lru_lambda, w_out, ln1_g, ln1_b, peer_wq, peer_keys, expert_u,
           expert_v, ln2_g, ln2_b):
    B, S, D = x.shape
    cos_t, sa_t, sb_t = _rotary_tables(S)
    for l in range(DEPTH):
        lambda_init = 0.8 - 0.6 * math.exp(-0.3 * l)
        q, k, v, xr, gate = _proj(x, w_in[l].astype(BF16), cos_t, sa_t, sb_t)
        attn = _attention(q, k, v, lambda_qk[l], subln_g[l], lambda_init)
        gate_w = jnp.stack([
            jnp.concatenate([_block_diag(gate_a_w[l, d]), _block_diag(gate_i_w[l, d])], axis=1)
            for d in range(2)]).astype(BF16)
        gate_b = jnp.concatenate([gate_a_b[l], gate_i_b[l]], axis=-1)[:, None, :]
        h = _lru(xr, conv_w[l], conv_b[l], gate_w, gate_b, lru_lambda[l][:, None, :])
        x1 = _outproj(x, attn, h, gate, w_out[l].astype(BF16), ln1_g[l], ln1_b[l])
        x1 = x1.reshape(B * S, D)
        cnt, ga, rank2, gb = _route(x1, peer_wq[l].astype(BF16), peer_keys[l].astype(BF16))
        x = _peer(x1, cnt, ga, rank2, gb, expert_u[l].astype(BF16),
                  expert_v[l].astype(BF16).T, ln2_g[l], ln2_b[l]).reshape(B, S, D)
    return x


def kernel(x_prompt, x_sample, w_in, lambda_qk, subln_g, conv_w, conv_b, gate_a_w, gate_a_b,
           gate_i_w, gate_i_b, lru_lambda, w_out, ln1_g, ln1_b, peer_wq, peer_keys, expert_u,
           expert_v, ln2_g, ln2_b):
    assert x_prompt.shape[1:] == x_sample.shape[1:]
    nb = x_prompt.shape[0]
    x = jnp.concatenate([x_prompt, x_sample], axis=0)
    y = _trunk(x, w_in, lambda_qk, subln_g, conv_w, conv_b, gate_a_w, gate_a_b, gate_i_w,
               gate_i_b, lru_lambda, w_out, ln1_g, ln1_b, peer_wq, peer_keys, expert_u,
               expert_v, ln2_g, ln2_b)
    return (y[:nb], y[nb:])
```

```python
import functools
import math

import jax
import jax.numpy as jnp
from jax import lax
from jax.experimental import pallas as pl
from jax.experimental.pallas import tpu as pltpu

F32 = jnp.float32
BF16 = jnp.bfloat16

D_MODEL = 1024
DEPTH = 2
ATTN_WIDTH = 512
LRU_WIDTH = 512
N_DIFF_HEADS = 4
QK_HEAD_DIM = 64
V_HEAD_DIM = 128
ROT_DIM = 16
ROPE_THETA = 500000.0
CONV_LEFT = 2
LRU_BLOCKS = 8
LRU_BLOCK_DIM = 64
LRU_C = 8.0
PEER_HEADS = 8
N_KEYS = 128
PEER_TOPK = 16
PEER_HALF = 128
IN_WIDTH = 3 * ATTN_WIDTH + 2 * LRU_WIDTH
DN_ALPHA = (2 * DEPTH) ** 0.25
LN_EPS = 1e-5
SUBLN_EPS = 1e-5

LANES = 128
SUBLANES = 8
NEG_INF = float("-inf")
LOG2_E = math.log2(math.e)

_NT = (((1,), (1,)), ((), ()))


def _tile(n, want):
    t = min(n, want)
    assert n % t == 0, (n, t)
    return t


def _layer_norm(y, g, b):
    mu = jnp.mean(y, axis=-1, keepdims=True)
    yc = y - mu
    var = jnp.mean(yc * yc, axis=-1, keepdims=True)
    return yc * lax.rsqrt(var + LN_EPS) * g + b


def _gelu(x):
    c = math.sqrt(2.0 / math.pi)
    hx = 0.5 * x
    return hx + hx * jnp.tanh(x * (c + (c * 0.044715) * (x * x)))


def _proj_body(x_ref, w_ref, cos_ref, sa_ref, sb_ref, q_ref, k_ref, v_ref, xr_ref, gate_ref):
    xb = x_ref[0].astype(BF16)
    cos = cos_ref[...]
    sa = sa_ref[...]
    sb = sb_ref[...]

    def rotate(t):
        return t * cos + pltpu.roll(t, 8, 1) * sa + pltpu.roll(t, LANES - 8, 1) * sb

    def proj(c0, width):
        return jnp.dot(xb, w_ref[:, c0:c0 + width], preferred_element_type=F32)

    q = proj(0, ATTN_WIDTH)
    k = proj(ATTN_WIDTH, ATTN_WIDTH)
    for h in range(N_DIFF_HEADS):
        cs = slice(h * LANES, (h + 1) * LANES)
        q_ref[0, :, cs] = (rotate(q[:, cs]) * (QK_HEAD_DIM ** -0.5 * LOG2_E)).astype(BF16)
        k_ref[0, :, cs] = rotate(k[:, cs]).astype(BF16)
    v_ref[0] = proj(2 * ATTN_WIDTH, ATTN_WIDTH).astype(BF16)
    xr_ref[0] = proj(3 * ATTN_WIDTH, LRU_WIDTH)
    gate_ref[0] = proj(3 * ATTN_WIDTH + LRU_WIDTH, LRU_WIDTH)


def _proj(x, w_in, cos_t, sa_t, sb_t):
    B, S, D = x.shape
    tm = _tile(S, 512)
    row = lambda b, i: (b, i, 0)
    tab = lambda b, i: (i, 0)
    out_bf = jax.ShapeDtypeStruct((B, S, ATTN_WIDTH), BF16)
    out_f = jax.ShapeDtypeStruct((B, S, LRU_WIDTH), F32)
    return pl.pallas_call(
        _proj_body,
        grid=(B, S // tm),
        in_specs=[
            pl.BlockSpec((1, tm, D), row),
            pl.BlockSpec((D, IN_WIDTH), lambda b, i: (0, 0)),
            pl.BlockSpec((tm, LANES), tab),
            pl.BlockSpec((tm, LANES), tab),
            pl.BlockSpec((tm, LANES), tab),
        ],
        out_specs=[pl.BlockSpec((1, tm, ATTN_WIDTH), row)] * 5,
        out_shape=[out_bf, out_bf, out_bf, out_f, out_f],
        compiler_params=pltpu.CompilerParams(
            dimension_semantics=("arbitrary", "arbitrary")),
        name="proj",
    )(x, w_in, cos_t, sa_t, sb_t)


def _rotary_tables(S):
    half = ROT_DIM // 2
    inv = ROPE_THETA ** (-jnp.arange(0, ROT_DIM, 2, dtype=F32) / ROT_DIM)
    ang = jnp.arange(S, dtype=F32)[:, None] * inv[None, :]
    cos, sin = jnp.cos(ang), jnp.sin(ang)
    ones = jnp.ones((S, QK_HEAD_DIM - ROT_DIM), F32)
    zeros = jnp.zeros((S, QK_HEAD_DIM - ROT_DIM), F32)
    z8 = jnp.zeros((S, half), F32)
    cos_t = jnp.concatenate([cos, cos, ones], axis=-1)
    sa_t = jnp.concatenate([z8, sin, zeros], axis=-1)
    sb_t = jnp.concatenate([-sin, z8, zeros], axis=-1)
    rep = lambda t: jnp.concatenate([t, t], axis=-1)
    return rep(cos_t), rep(sa_t), rep(sb_t)


def _attn_body(lam_ref, g_ref, q_ref, k_ref, v_ref, o_ref, km_sc, va_sc, m_sc, acc_sc, *,
               lambda_init, tk):
    seq = k_ref.shape[1]

    @pl.when(pl.program_id(2) == 0)
    def _():
        k = k_ref[0]
        lane = lax.broadcasted_iota(jnp.int32, k.shape, 1)
        zero = jnp.zeros_like(k)
        km_sc[0] = jnp.where(lane < QK_HEAD_DIM, k, zero)
        km_sc[1] = jnp.where(lane >= QK_HEAD_DIM, k, zero)
        va_sc[:, :V_HEAD_DIM] = v_ref[0]
        va_sc[:, V_HEAD_DIM:] = jnp.ones((seq, V_HEAD_DIM), BF16)

    m_sc[...] = jnp.full(m_sc.shape, NEG_INF, F32)
    acc_sc[...] = jnp.zeros(acc_sc.shape, F32)
    q = q_ref[0]

    def chunk(j, carry):
        ks = pl.ds(pl.multiple_of(j * tk, tk), tk)
        for c in range(2):
            s = lax.dot_general(q, km_sc[c, ks, :], _NT, preferred_element_type=F32)
            m_prev = m_sc[c]
            m_new = jnp.maximum(m_prev, jnp.max(s, axis=1, keepdims=True))
            alpha = jnp.exp2(m_prev - m_new)
            p = jnp.exp2(s - jnp.tile(m_new, (1, tk // LANES)))
            pv = jnp.dot(p.astype(BF16), va_sc[ks, :], preferred_element_type=F32)
            acc_sc[c] = jnp.tile(alpha, (1, 2)) * acc_sc[c] + pv
            m_sc[c] = m_new
        return carry

    lax.fori_loop(0, seq // tk, chunk, 0, unroll=True)

    lp = lam_ref[...]
    d1 = jnp.sum(lp[0:1] * lp[1:2], axis=1, keepdims=True)
    d2 = jnp.sum(lp[2:3] * lp[3:4], axis=1, keepdims=True)
    lam = jnp.exp(d1) - jnp.exp(d2) + lambda_init
    acc1 = acc_sc[0]
    acc2 = acc_sc[1]
    o = (acc1[:, :V_HEAD_DIM] / acc1[:, V_HEAD_DIM:]
         - lam * (acc2[:, :V_HEAD_DIM] / acc2[:, V_HEAD_DIM:]))
    o = o * lax.rsqrt(jnp.mean(o * o, axis=1, keepdims=True) + SUBLN_EPS) * g_ref[...]
    o_ref[0] = (o * (1.0 - lambda_init)).astype(o_ref.dtype)


def _attention(q, k, v, lambda_qk, subln_g, lambda_init):
    B, S, _ = q.shape
    tq = _tile(S, 1024)
    tk = _tile(S, 1024)
    return pl.pallas_call(
        functools.partial(_attn_body, lambda_init=lambda_init, tk=tk),
        grid=(B, N_DIFF_HEADS, S // tq),
        in_specs=[
            pl.BlockSpec((4, QK_HEAD_DIM), lambda b, h, i: (0, 0)),
            pl.BlockSpec((1, V_HEAD_DIM), lambda b, h, i: (0, 0)),
            pl.BlockSpec((1, tq, LANES), lambda b, h, i: (b, i, h)),
            pl.BlockSpec((1, S, LANES), lambda b, h, i: (b, 0, h)),
            pl.BlockSpec((1, S, LANES), lambda b, h, i: (b, 0, h)),
        ],
        out_specs=pl.BlockSpec((1, tq, LANES), lambda b, h, i: (b, i, h)),
        out_shape=jax.ShapeDtypeStruct((B, S, ATTN_WIDTH), BF16),
        scratch_shapes=[
            pltpu.VMEM((2, S, LANES), BF16),
            pltpu.VMEM((S, 2 * V_HEAD_DIM), BF16),
            pltpu.VMEM((2, tq, LANES), F32),
            pltpu.VMEM((2, tq, 2 * V_HEAD_DIM), F32),
        ],
        compiler_params=pltpu.CompilerParams(
            dimension_semantics=("arbitrary", "arbitrary", "arbitrary")),
        name="attn",
    )(lambda_qk, subln_g.reshape(1, V_HEAD_DIM), q, k, v)


def _lru_body(x_ref, prev_ref, next_ref, cw_ref, cb_ref, gw_ref, gb_ref, lam_ref, h_ref,
              a_sc, b_sc, carry_sc):
    d = pl.program_id(1)
    j = pl.program_id(2)
    n_t = pl.num_programs(2)
    tt = x_ref.shape[1]
    jj = jnp.where(d == 0, j, n_t - 1 - j)

    @pl.when(j == 0)
    def _():
        carry_sc[...] = jnp.zeros(carry_sc.shape, F32)

    x = x_ref[0]
    prev = prev_ref[0] * (jj > 0).astype(F32)
    nxt = next_ref[0] * (jj < n_t - 1).astype(F32)
    row = lax.broadcasted_iota(jnp.int32, x.shape, 0)
    cw = cw_ref[...]
    x_m1 = jnp.where(row == 0, prev[7:8], pltpu.roll(x, 1, 0))
    x_m2 = jnp.where(row == 0, prev[6:7], jnp.where(row == 1, prev[7:8], pltpu.roll(x, 2, 0)))
    x_p1 = jnp.where(row == tt - 1, nxt[0:1], pltpu.roll(x, tt - 1, 0))
    xc = x_m2 * cw[0:1] + x_m1 * cw[1:2] + x * cw[2:3] + x_p1 * cw[3:4] + cb_ref[...]

    gates = jnp.dot(xc.astype(BF16), gw_ref[0], preferred_element_type=F32) + gb_ref[0]
    r = jax.nn.sigmoid(gates[:, :LRU_WIDTH])
    i = jax.nn.sigmoid(gates[:, LRU_WIDTH:])
    lam = lam_ref[0]
    softplus_neg = jnp.maximum(-lam, 0.0) + jnp.log(1.0 + jnp.exp(-jnp.abs(lam)))
    log_a = -LRU_C * r * softplus_neg
    a = jnp.exp(log_a)
    b = jnp.sqrt(1.0 - a * a) * (i * xc)

    sub = row & (SUBLANES - 1)

    def local_scan(a, b, reverse):
        shift = 1
        while shift < SUBLANES:
            if reverse:
                a_s = pltpu.roll(a, tt - shift, 0)
                b_s = pltpu.roll(b, tt - shift, 0)
                valid = sub < SUBLANES - shift
            else:
                a_s = pltpu.roll(a, shift, 0)
                b_s = pltpu.roll(b, shift, 0)
                valid = sub >= shift
            b = a * jnp.where(valid, b_s, 0.0) + b
            a = a * jnp.where(valid, a_s, 1.0)
            shift *= 2
        return a, b

    n_groups = tt // SUBLANES

    def sweep(reverse):
        a_l, b_l = local_scan(a, b, reverse)
        a_sc[...] = a_l
        b_sc[...] = b_l

        def step(g, carry):
            gi = (n_groups - 1 - g) if reverse else g
            rs = pl.ds(pl.multiple_of(gi * SUBLANES, SUBLANES), SUBLANES)
            h = b_sc[rs, :] + a_sc[rs, :] * carry
            h_ref[0, 0, rs, :] = h
            return h[0:1] if reverse else h[SUBLANES - 1:SUBLANES]

        carry_sc[...] = lax.fori_loop(0, n_groups, step, carry_sc[...])

    @pl.when(d == 0)
    def _():
        sweep(False)

    @pl.when(d == 1)
    def _():
        sweep(True)


def _lru(xr, conv_w, conv_b, gate_w, gate_b, lru_lam):
    B, S, W = xr.shape
    tt = _tile(S, 1024)
    n_t = S // tt
    hb = tt // SUBLANES
    n_hb = S // SUBLANES

    def tblk(d, j):
        return jnp.where(d == 0, j, n_t - 1 - j)

    return pl.pallas_call(
        _lru_body,
        grid=(B, 2, n_t),
        in_specs=[
            pl.BlockSpec((1, tt, W), lambda b, d, j: (b, tblk(d, j), 0)),
            pl.BlockSpec((1, SUBLANES, W),
                         lambda b, d, j: (b, jnp.maximum(tblk(d, j) * hb - 1, 0), 0)),
            pl.BlockSpec((1, SUBLANES, W),
                         lambda b, d, j: (b, jnp.minimum((tblk(d, j) + 1) * hb, n_hb - 1), 0)),
            pl.BlockSpec((4, W), lambda b, d, j: (0, 0)),
            pl.BlockSpec((1, W), lambda b, d, j: (0, 0)),
            pl.BlockSpec((1, W, 2 * W), lambda b, d, j: (d, 0, 0)),
            pl.BlockSpec((1, 1, 2 * W), lambda b, d, j: (d, 0, 0)),
            pl.BlockSpec((1, 1, W), lambda b, d, j: (d, 0, 0)),
        ],
        out_specs=pl.BlockSpec((1, 1, tt, W), lambda b, d, j: (b, d, tblk(d, j), 0)),
        out_shape=jax.ShapeDtypeStruct((B, 2, S, W), F32),
        scratch_shapes=[
            pltpu.VMEM((tt, W), F32),
            pltpu.VMEM((tt, W), F32),
            pltpu.VMEM((1, W), F32),
        ],
        compiler_params=pltpu.CompilerParams(
            dimension_semantics=("arbitrary", "arbitrary", "arbitrary")),
        name="lru",
    )(xr, xr, xr, conv_w, conv_b.reshape(1, W), gate_w, gate_b, lru_lam)


def _block_diag(w):
    eye = jnp.eye(LRU_BLOCKS, dtype=w.dtype)
    return jnp.einsum('nde,nm->ndme', w, eye).reshape(LRU_WIDTH, LRU_WIDTH)


def _outproj_body(x_ref, attn_ref, hf_ref, hb_ref, gate_ref, wo_ref, g_ref, b_ref, o_ref):
    rec = (hf_ref[0, 0] + hb_ref[0, 0]) * _gelu(gate_ref[0])
    mix = jnp.dot(attn_ref[0], wo_ref[:ATTN_WIDTH], preferred_element_type=F32)
    mix += jnp.dot(rec.astype(BF16), wo_ref[ATTN_WIDTH:], preferred_element_type=F32)
    o_ref[0] = _layer_norm(DN_ALPHA * x_ref[0] + mix, g_ref[...], b_ref[...])


def _outproj(x, attn, h, gate, w_out, ln_g, ln_b):
    B, S, D = x.shape
    tm = _tile(S, 512)
    row = lambda b, i: (b, i, 0)
    const = lambda b, i: (0, 0)
    return pl.pallas_call(
        _outproj_body,
        grid=(B, S // tm),
        in_specs=[
            pl.BlockSpec((1, tm, D), row),
            pl.BlockSpec((1, tm, ATTN_WIDTH), row),
            pl.BlockSpec((1, 1, tm, LRU_WIDTH), lambda b, i: (b, 0, i, 0)),
            pl.BlockSpec((1, 1, tm, LRU_WIDTH), lambda b, i: (b, 1, i, 0)),
            pl.BlockSpec((1, tm, LRU_WIDTH), row),
            pl.BlockSpec((ATTN_WIDTH + LRU_WIDTH, D), const),
            pl.BlockSpec((1, D), const),
            pl.BlockSpec((1, D), const),
        ],
        out_specs=pl.BlockSpec((1, tm, D), row),
        out_shape=jax.ShapeDtypeStruct((B, S, D), F32),
        compiler_params=pltpu.CompilerParams(
            dimension_semantics=("arbitrary", "arbitrary")),
        name="outproj",
    )(x, attn, h, h, gate, w_out, ln_g.reshape(1, D), ln_b.reshape(1, D))


_CAND_PAIRS = [(i, j) for i in range(PEER_TOPK + 1) for j in range(PEER_TOPK + 1)
               if (i + 1) * (j + 1) <= PEER_TOPK + 1]
_CAND_ROWS = -(-len(_CAND_PAIRS) // SUBLANES) * SUBLANES


def _top_rows(s, n, want_rank):
    rows = []
    cur = s
    rank = jnp.full(s.shape, float(n), F32) if want_rank else None
    for r in range(n):
        mx = jnp.max(cur, axis=0, keepdims=True)
        rows.append(mx)
        hit = cur == mx
        if want_rank:
            rank = jnp.where(hit, float(r), rank)
        if r < n - 1:
            cur = jnp.where(hit, NEG_INF, cur)
    return rows, rank


def _dup_bf16(x):
    u = pltpu.bitcast(x.astype(BF16).astype(F32), jnp.uint32)
    return u | lax.shift_right_logical(u, jnp.uint32(16))


def _route_body(x_ref, wq_ref, keys_ref, cnt_ref, ga_ref, rank_ref, gb_ref, cand_sc):
    xb = x_ref[...].astype(BF16)
    qp = jnp.dot(xb, wq_ref[...], preferred_element_type=F32)
    cand_sc[...] = jnp.full(cand_sc.shape, NEG_INF, F32)
    n = PEER_TOPK + 1
    for h in range(PEER_HEADS):
        s = []
        for p in range(2):
            c0 = (2 * h + p) * PEER_HALF
            blk = qp[:, c0:c0 + PEER_HALF].astype(BF16)
            s.append(lax.dot_general(keys_ref[p], blk, _NT, preferred_element_type=F32))
        s1, s2 = s
        a, _ = _top_rows(s1, n, False)
        b, rank2 = _top_rows(s2, n, True)
        for r, (i, j) in enumerate(_CAND_PAIRS):
            cand_sc[r:r + 1, :] = a[i] + b[j]
        cand = cand_sc[...]
        cur = cand
        for r in range(PEER_TOPK):
            mx = jnp.max(cur, axis=0, keepdims=True)
            cur = jnp.where(cur == mx, NEG_INF, cur)
        thr = 0.5 * (mx + jnp.max(cur, axis=0, keepdims=True))
        z = jnp.sum(jnp.where(cand >= thr, jnp.exp(cand - (a[0] + b[0])), 0.0),
                    axis=0, keepdims=True)
        cnt = jnp.zeros(s1.shape, F32)
        for j in range(n):
            cnt += jnp.where(s1 + b[j] >= thr, 1.0, 0.0)
        outs = (
            (cnt_ref, _dup_bf16(cnt)),
            (ga_ref, _dup_bf16(jnp.exp(s1 - a[0]) / z)),
            (rank_ref, pltpu.bitcast(rank2.astype(BF16), jnp.uint32)),
            (gb_ref, pltpu.bitcast(jnp.exp(s2 - b[0]).astype(BF16), jnp.uint32)),
        )
        for ref, val in outs:
            for c in range(val.shape[1] // LANES):
                ref[c, h] = val[:, c * LANES:(c + 1) * LANES]


def _route(x1, wq, keys):
    T, D = x1.shape
    tm = _tile(T, 256)
    shape = (T // LANES, PEER_HEADS, N_KEYS, LANES)
    packed = (T // LANES, PEER_HEADS, N_KEYS // 2, LANES)
    ospec = pl.BlockSpec((tm // LANES,) + shape[1:], lambda i: (i, 0, 0, 0))
    pspec = pl.BlockSpec((tm // LANES,) + packed[1:], lambda i: (i, 0, 0, 0))
    return pl.pallas_call(
        _route_body,
        grid=(T // tm,),
        in_specs=[
            pl.BlockSpec((tm, D), lambda i: (i, 0)),
            pl.BlockSpec((D, 2 * PEER_HEADS * PEER_HALF), lambda i: (0, 0)),
            pl.BlockSpec((2, N_KEYS, PEER_HALF), lambda i: (0, 0, 0)),
        ],
        out_specs=[ospec, ospec, pspec, pspec],
        out_shape=[jax.ShapeDtypeStruct(shape, jnp.uint32), jax.ShapeDtypeStruct(shape, jnp.uint32),
                   jax.ShapeDtypeStruct(packed, jnp.uint32), jax.ShapeDtypeStruct(packed, jnp.uint32)],
        scratch_shapes=[pltpu.VMEM((_CAND_ROWS, tm), F32)],
        compiler_params=pltpu.CompilerParams(dimension_semantics=("arbitrary",)),
        name="route",
    )(x1, wq, keys)


PEER_SUB = 1024
PEER_SUBS_PER_STEP = 4
PEER_TOK = 256
PEER_SLAB = 16


def _peer_body(x_ref, cnt_ref, ga_ref, rank_ref, gb_ref, eu_ref, evt_ref, g_ref, b_ref, o_ref,
               xb_sc, act0_sc, act1_sc, wg0_sc, wg1_sc, acc_sc):
    j = pl.program_id(1)
    n_half = xb_sc.shape[0]
    assert n_half == 2
    act_sc = (act0_sc, act1_sc)
    wg_sc = (wg0_sc, wg1_sc)

    @pl.when(j == 0)
    def _():
        for t in range(n_half):
            xb_sc[t] = x_ref[t * PEER_TOK:(t + 1) * PEER_TOK, :].astype(BF16)
        acc_sc[...] = jnp.zeros(acc_sc.shape, F32)

    def stage_a(s, t):
        act_sc[t][...] = lax.dot_general(eu_ref[s], xb_sc[t], _NT, preferred_element_type=F32)

    def stage_c(s, t):
        acc_sc[t] += jnp.dot(evt_ref[s], wg_sc[t][...], preferred_element_type=F32)

    def rows16(ref, ch, h, g):
        row = ref[ch, h, pl.ds(g, 1), :]
        return pltpu.bitcast(jnp.broadcast_to(row, (SUBLANES, LANES)), BF16)

    def stage_b(s, t):
        for gg in range(PEER_SUB // N_KEYS):
            g = s * (PEER_SUB // N_KEYS) + gg
            for lc in range(PEER_TOK // LANES):
                ch = t * (PEER_TOK // LANES) + lc
                ls = slice(lc * LANES, (lc + 1) * LANES)
                cnt = [rows16(cnt_ref, ch, h, g) for h in range(PEER_HEADS)]
                ga = [rows16(ga_ref, ch, h, g) for h in range(PEER_HEADS)]
                for sl in range(N_KEYS // PEER_SLAB):
                    rs = slice(sl * SUBLANES, (sl + 1) * SUBLANES)
                    es = slice(gg * N_KEYS + sl * PEER_SLAB, gg * N_KEYS + (sl + 1) * PEER_SLAB)
                    w = None
                    for h in range(PEER_HEADS):
                        rank = pltpu.bitcast(rank_ref[ch, h, rs, :], BF16)
                        gb = pltpu.bitcast(gb_ref[ch, h, rs, :], BF16)
                        term = jnp.where(rank < cnt[h], ga[h], 0.0) * gb
                        w = term if w is None else w + term
                    wg_sc[t][es, ls] = w * _gelu(act_sc[t][es, ls]).astype(BF16)

    n_sub = PEER_SUBS_PER_STEP
    stage_a(0, 0)
    stage_a(0, 1)
    stage_b(0, 0)

    def steady(k, carry):
        stage_a(k + 1, 0)
        stage_b(k, 1)
        stage_c(k, 0)
        stage_a(k + 1, 1)
        stage_b(k + 1, 0)
        stage_c(k, 1)
        return carry

    lax.fori_loop(0, n_sub - 1, steady, 0)
    stage_b(n_sub - 1, 1)
    stage_c(n_sub - 1, 0)
    stage_c(n_sub - 1, 1)

    @pl.when(j == pl.num_programs(1) - 1)
    def _():
        for t in range(n_half):
            ts = slice(t * PEER_TOK, (t + 1) * PEER_TOK)
            y = DN_ALPHA * x_ref[ts, :] + acc_sc[t].T
            o_ref[ts, :] = _layer_norm(y, g_ref[...], b_ref[...])


def _peer(x1, cnt, ga, rank2, gb, eu, ev, ln_g, ln_b):
    T, D = x1.shape
    n_exp = eu.shape[0]
    tc = _tile(T, 2 * PEER_TOK)
    assert tc == 2 * PEER_TOK
    ns = PEER_SUBS_PER_STEP
    ne = ns * PEER_SUB
    gs = ne // N_KEYS
    nch = tc // LANES
    eu3 = eu.reshape(n_exp // PEER_SUB, PEER_SUB, D)
    evt3 = ev.reshape(n_exp // PEER_SUB, PEER_SUB, D).transpose(0, 2, 1)
    per_group = pl.BlockSpec((nch, PEER_HEADS, gs, LANES), lambda i, j: (i, 0, j, 0))
    per_key = pl.BlockSpec((nch, PEER_HEADS, N_KEYS // 2, LANES), lambda i, j: (i, 0, 0, 0))
    const = lambda i, j: (0, 0)
    return pl.pallas_call(
        _peer_body,
        grid=(T // tc, n_exp // ne),
        in_specs=[
            pl.BlockSpec((tc, D), lambda i, j: (i, 0)),
            per_group, per_group, per_key, per_key,
            pl.BlockSpec((ns, PEER_SUB, D), lambda i, j: (j, 0, 0)),
            pl.BlockSpec((ns, D, PEER_SUB), lambda i, j: (j, 0, 0)),
            pl.BlockSpec((1, D), const),
            pl.BlockSpec((1, D), const),
        ],
        out_specs=pl.BlockSpec((tc, D), lambda i, j: (i, 0)),
        out_shape=jax.ShapeDtypeStruct((T, D), F32),
        scratch_shapes=[
            pltpu.VMEM((2, PEER_TOK, D), BF16),
            pltpu.VMEM((PEER_SUB, PEER_TOK), F32),
            pltpu.VMEM((PEER_SUB, PEER_TOK), F32),
            pltpu.VMEM((PEER_SUB, PEER_TOK), BF16),
            pltpu.VMEM((PEER_SUB, PEER_TOK), BF16),
            pltpu.VMEM((2, D, PEER_TOK), F32),
        ],
        compiler_params=pltpu.CompilerParams(
            dimension_semantics=("arbitrary", "arbitrary")),
        name="peer",
    )(x1, cnt, ga, rank2, gb, eu3, evt3, ln_g.reshape(1, D), ln_b.reshape(1, D))


def _trunk(x, w_in, lambda_qk, subln_g, conv_w, conv_b, gate_a_w, gate_a_b, gate_i_w,
           gate_i_b, lru_lambda, w_out, ln1_g, ln1_b, peer_wq, peer_keys, expert_u,
           expert_v, ln2_g, ln2_b):
    B, S, D = x.shape
    cos_t, sa_t, sb_t = _rotary_tables(S)
    for l in range(DEPTH):
        lambda_init = 0.8 - 0.6 * math.exp(-0.3 * l)
        q, k, v, xr, gate = _proj(x, w_in[l].astype(BF16), cos_t, sa_t, sb_t)
        attn = _attention(q, k, v, lambda_qk[l], subln_g[l], lambda_init)
        gate_w = jnp.stack([
            jnp.concatenate([_block_diag(gate_a_w[l, d]), _block_diag(gate_i_w[l, d])], axis=1)
            for d in range(2)]).astype(BF16)
        gate_b = jnp.concatenate([gate_a_b[l], gate_i_b[l]], axis=-1)[:, None, :]
        h = _lru(xr, conv_w[l], conv_b[l], gate_w, gate_b, lru_lambda[l][:, None, :])
        x1 = _outproj(x, attn, h, gate, w_out[l].astype(BF16), ln1_g[l], ln1_b[l])
        x1 = x1.reshape(B * S, D)
        cnt, ga, rank2, gb = _route(x1, peer_wq[l].astype(BF16), peer_keys[l].astype(BF16))
        x = _peer(x1, cnt, ga, rank2, gb, expert_u[l].astype(BF16),
                  expert_v[l].astype(BF16), ln2_g[l], ln2_b[l]).reshape(B, S, D)
    return x


def kernel(x_prompt, x_sample, w_in, lambda_qk, subln_g, conv_w, conv_b, gate_a_w, gate_a_b,
           gate_i_w, gate_i_b, lru_lambda, w_out, ln1_g, ln1_b, peer_wq, peer_keys, expert_u,
           expert_v, ln2_g, ln2_b):
    assert x_prompt.shape[1:] == x_sample.shape[1:]
    nb = x_prompt.shape[0]
    x = jnp.concatenate([x_prompt, x_sample], axis=0)
    y = _trunk(x, w_in, lambda_qk, subln_g, conv_w, conv_b, gate_a_w, gate_a_b, gate_i_w,
               gate_i_b, lru_lambda, w_out, ln1_g, ln1_b, peer_wq, peer_keys, expert_u,
               expert_v, ln2_g, ln2_b)
    return (y[:nb], y[nb:])
```

```python
import functools
import math

import jax
import jax.numpy as jnp
from jax import lax
from jax.experimental import pallas as pl
from jax.experimental.pallas import tpu as pltpu

F32 = jnp.float32
BF16 = jnp.bfloat16

D_MODEL = 1024
DEPTH = 2
ATTN_WIDTH = 512
LRU_WIDTH = 512
N_DIFF_HEADS = 4
QK_HEAD_DIM = 64
V_HEAD_DIM = 128
ROT_DIM = 16
ROPE_THETA = 500000.0
CONV_LEFT = 2
LRU_BLOCKS = 8
LRU_BLOCK_DIM = 64
LRU_C = 8.0
PEER_HEADS = 8
N_KEYS = 128
PEER_TOPK = 16
PEER_HALF = 128
IN_WIDTH = 3 * ATTN_WIDTH + 2 * LRU_WIDTH
DN_ALPHA = (2 * DEPTH) ** 0.25
LN_EPS = 1e-5
SUBLN_EPS = 1e-5

LANES = 128
SUBLANES = 8
NEG_INF = float("-inf")
LOG2_E = math.log2(math.e)

_NT = (((1,), (1,)), ((), ()))


def _tile(n, want):
    t = min(n, want)
    assert n % t == 0, (n, t)
    return t


def _layer_norm(y, g, b):
    mu = jnp.mean(y, axis=-1, keepdims=True)
    yc = y - mu
    var = jnp.mean(yc * yc, axis=-1, keepdims=True)
    return yc * lax.rsqrt(var + LN_EPS) * g + b


def _gelu(x):
    c = math.sqrt(2.0 / math.pi)
    hx = 0.5 * x
    return hx + hx * jnp.tanh(x * (c + (c * 0.044715) * (x * x)))


def _proj_body(x_ref, w_ref, cos_ref, sa_ref, sb_ref, q_ref, k_ref, v_ref, xr_ref, gate_ref):
    xb = x_ref[0].astype(BF16)
    cos = cos_ref[...]
    sa = sa_ref[...]
    sb = sb_ref[...]

    def rotate(t):
        return t * cos + pltpu.roll(t, 8, 1) * sa + pltpu.roll(t, LANES - 8, 1) * sb

    def proj(c0, width):
        return jnp.dot(xb, w_ref[:, c0:c0 + width], preferred_element_type=F32)

    q = proj(0, ATTN_WIDTH)
    k = proj(ATTN_WIDTH, ATTN_WIDTH)
    for h in range(N_DIFF_HEADS):
        cs = slice(h * LANES, (h + 1) * LANES)
        q_ref[0, :, cs] = (rotate(q[:, cs]) * (QK_HEAD_DIM ** -0.5 * LOG2_E)).astype(BF16)
        k_ref[0, :, cs] = rotate(k[:, cs]).astype(BF16)
    v_ref[0] = proj(2 * ATTN_WIDTH, ATTN_WIDTH).astype(BF16)
    xr_ref[0] = proj(3 * ATTN_WIDTH, LRU_WIDTH)
    gate_ref[0] = proj(3 * ATTN_WIDTH + LRU_WIDTH, LRU_WIDTH)


def _proj(x, w_in, cos_t, sa_t, sb_t):
    B, S, D = x.shape
    tm = _tile(S, 512)
    row = lambda b, i: (b, i, 0)
    tab = lambda b, i: (i, 0)
    out_bf = jax.ShapeDtypeStruct((B, S, ATTN_WIDTH), BF16)
    out_f = jax.ShapeDtypeStruct((B, S, LRU_WIDTH), F32)
    return pl.pallas_call(
        _proj_body,
        grid=(B, S // tm),
        in_specs=[
            pl.BlockSpec((1, tm, D), row),
            pl.BlockSpec((D, IN_WIDTH), lambda b, i: (0, 0)),
            pl.BlockSpec((tm, LANES), tab),
            pl.BlockSpec((tm, LANES), tab),
            pl.BlockSpec((tm, LANES), tab),
        ],
        out_specs=[pl.BlockSpec((1, tm, ATTN_WIDTH), row)] * 5,
        out_shape=[out_bf, out_bf, out_bf, out_f, out_f],
        compiler_params=pltpu.CompilerParams(
            dimension_semantics=("arbitrary", "arbitrary")),
        name="proj",
    )(x, w_in, cos_t, sa_t, sb_t)


def _rotary_tables(S):
    half = ROT_DIM // 2
    inv = ROPE_THETA ** (-jnp.arange(0, ROT_DIM, 2, dtype=F32) / ROT_DIM)
    ang = jnp.arange(S, dtype=F32)[:, None] * inv[None, :]
    cos, sin = jnp.cos(ang), jnp.sin(ang)
    ones = jnp.ones((S, QK_HEAD_DIM - ROT_DIM), F32)
    zeros = jnp.zeros((S, QK_HEAD_DIM - ROT_DIM), F32)
    z8 = jnp.zeros((S, half), F32)
    cos_t = jnp.concatenate([cos, cos, ones], axis=-1)
    sa_t = jnp.concatenate([z8, sin, zeros], axis=-1)
    sb_t = jnp.concatenate([-sin, z8, zeros], axis=-1)
    rep = lambda t: jnp.concatenate([t, t], axis=-1)
    return rep(cos_t), rep(sa_t), rep(sb_t)


def _attn_body(lam_ref, g_ref, q_ref, k_ref, v_ref, o_ref, km_sc, va_sc, m_sc, acc_sc, *,
               lambda_init, tk):
    seq = k_ref.shape[1]

    @pl.when(pl.program_id(2) == 0)
    def _():
        k = k_ref[0]
        lane = lax.broadcasted_iota(jnp.int32, k.shape, 1)
        zero = jnp.zeros_like(k)
        km_sc[0] = jnp.where(lane < QK_HEAD_DIM, k, zero)
        km_sc[1] = jnp.where(lane >= QK_HEAD_DIM, k, zero)
        va_sc[:, :V_HEAD_DIM] = v_ref[0]
        va_sc[:, V_HEAD_DIM:] = jnp.ones((seq, V_HEAD_DIM), BF16)

    m_sc[...] = jnp.full(m_sc.shape, NEG_INF, F32)
    acc_sc[...] = jnp.zeros(acc_sc.shape, F32)
    q = q_ref[0]

    def chunk(j, carry):
        ks = pl.ds(pl.multiple_of(j * tk, tk), tk)
        for c in range(2):
            s = lax.dot_general(q, km_sc[c, ks, :], _NT, preferred_element_type=F32)
            m_prev = m_sc[c]
            m_new = jnp.maximum(m_prev, jnp.max(s, axis=1, keepdims=True))
            alpha = jnp.exp2(m_prev - m_new)
            p = jnp.exp2(s - jnp.tile(m_new, (1, tk // LANES)))
            pv = jnp.dot(p.astype(BF16), va_sc[ks, :], preferred_element_type=F32)
            acc_sc[c] = jnp.tile(alpha, (1, 2)) * acc_sc[c] + pv
            m_sc[c] = m_new
        return carry

    lax.fori_loop(0, seq // tk, chunk, 0, unroll=True)

    lp = lam_ref[...]
    d1 = jnp.sum(lp[0:1] * lp[1:2], axis=1, keepdims=True)
    d2 = jnp.sum(lp[2:3] * lp[3:4], axis=1, keepdims=True)
    lam = jnp.exp(d1) - jnp.exp(d2) + lambda_init
    acc1 = acc_sc[0]
    acc2 = acc_sc[1]
    o = (acc1[:, :V_HEAD_DIM] / acc1[:, V_HEAD_DIM:]
         - lam * (acc2[:, :V_HEAD_DIM] / acc2[:, V_HEAD_DIM:]))
    o = o * lax.rsqrt(jnp.mean(o * o, axis=1, keepdims=True) + SUBLN_EPS) * g_ref[...]
    o_ref[0] = (o * (1.0 - lambda_init)).astype(o_ref.dtype)


def _attention(q, k, v, lambda_qk, subln_g, lambda_init):
    B, S, _ = q.shape
    tq = _tile(S, 1024)
    tk = _tile(S, 1024)
    return pl.pallas_call(
        functools.partial(_attn_body, lambda_init=lambda_init, tk=tk),
        grid=(B, N_DIFF_HEADS, S // tq),
        in_specs=[
            pl.BlockSpec((4, QK_HEAD_DIM), lambda b, h, i: (0, 0)),
            pl.BlockSpec((1, V_HEAD_DIM), lambda b, h, i: (0, 0)),
            pl.BlockSpec((1, tq, LANES), lambda b, h, i: (b, i, h)),
            pl.BlockSpec((1, S, LANES), lambda b, h, i: (b, 0, h)),
            pl.BlockSpec((1, S, LANES), lambda b, h, i: (b, 0, h)),
        ],
        out_specs=pl.BlockSpec((1, tq, LANES), lambda b, h, i: (b, i, h)),
        out_shape=jax.ShapeDtypeStruct((B, S, ATTN_WIDTH), BF16),
        scratch_shapes=[
            pltpu.VMEM((2, S, LANES), BF16),
            pltpu.VMEM((S, 2 * V_HEAD_DIM), BF16),
            pltpu.VMEM((2, tq, LANES), F32),
            pltpu.VMEM((2, tq, 2 * V_HEAD_DIM), F32),
        ],
        compiler_params=pltpu.CompilerParams(
            dimension_semantics=("arbitrary", "arbitrary", "arbitrary")),
        name="attn",
    )(lambda_qk, subln_g.reshape(1, V_HEAD_DIM), q, k, v)


def _lru_body(x_ref, prev_ref, next_ref, cw_ref, cb_ref, gw_ref, gb_ref, lam_ref, h_ref,
              a_sc, b_sc, carry_sc):
    d = pl.program_id(1)
    j = pl.program_id(2)
    n_t = pl.num_programs(2)
    tt = x_ref.shape[1]
    jj = jnp.where(d == 0, j, n_t - 1 - j)

    @pl.when(j == 0)
    def _():
        carry_sc[...] = jnp.zeros(carry_sc.shape, F32)

    x = x_ref[0]
    prev = prev_ref[0] * (jj > 0).astype(F32)
    nxt = next_ref[0] * (jj < n_t - 1).astype(F32)
    row = lax.broadcasted_iota(jnp.int32, x.shape, 0)
    cw = cw_ref[...]
    x_m1 = jnp.where(row == 0, prev[7:8], pltpu.roll(x, 1, 0))
    x_m2 = jnp.where(row == 0, prev[6:7], jnp.where(row == 1, prev[7:8], pltpu.roll(x, 2, 0)))
    x_p1 = jnp.where(row == tt - 1, nxt[0:1], pltpu.roll(x, tt - 1, 0))
    xc = x_m2 * cw[0:1] + x_m1 * cw[1:2] + x * cw[2:3] + x_p1 * cw[3:4] + cb_ref[...]

    gates = jnp.dot(xc.astype(BF16), gw_ref[0], preferred_element_type=F32) + gb_ref[0]
    r = jax.nn.sigmoid(gates[:, :LRU_WIDTH])
    i = jax.nn.sigmoid(gates[:, LRU_WIDTH:])
    lam = lam_ref[0]
    softplus_neg = jnp.maximum(-lam, 0.0) + jnp.log(1.0 + jnp.exp(-jnp.abs(lam)))
    log_a = -LRU_C * r * softplus_neg
    a = jnp.exp(log_a)
    b = jnp.sqrt(1.0 - a * a) * (i * xc)

    sub = row & (SUBLANES - 1)

    def local_scan(a, b, reverse):
        shift = 1
        while shift < SUBLANES:
            if reverse:
                a_s = pltpu.roll(a, tt - shift, 0)
                b_s = pltpu.roll(b, tt - shift, 0)
                valid = sub < SUBLANES - shift
            else:
                a_s = pltpu.roll(a, shift, 0)
                b_s = pltpu.roll(b, shift, 0)
                valid = sub >= shift
            b = a * jnp.where(valid, b_s, 0.0) + b
            a = a * jnp.where(valid, a_s, 1.0)
            shift *= 2
        return a, b

    n_groups = tt // SUBLANES

    def sweep(reverse):
        a_l, b_l = local_scan(a, b, reverse)
        a_sc[...] = a_l
        b_sc[...] = b_l

        def step(g, carry):
            gi = (n_groups - 1 - g) if reverse else g
            rs = pl.ds(pl.multiple_of(gi * SUBLANES, SUBLANES), SUBLANES)
            h = b_sc[rs, :] + a_sc[rs, :] * carry
            h_ref[0, 0, rs, :] = h
            return h[0:1] if reverse else h[SUBLANES - 1:SUBLANES]

        carry_sc[...] = lax.fori_loop(0, n_groups, step, carry_sc[...])

    @pl.when(d == 0)
    def _():
        sweep(False)

    @pl.when(d == 1)
    def _():
        sweep(True)


def _lru(xr, conv_w, conv_b, gate_w, gate_b, lru_lam):
    B, S, W = xr.shape
    tt = _tile(S, 1024)
    n_t = S // tt
    hb = tt // SUBLANES
    n_hb = S // SUBLANES

    def tblk(d, j):
        return jnp.where(d == 0, j, n_t - 1 - j)

    return pl.pallas_call(
        _lru_body,
        grid=(B, 2, n_t),
        in_specs=[
            pl.BlockSpec((1, tt, W), lambda b, d, j: (b, tblk(d, j), 0)),
            pl.BlockSpec((1, SUBLANES, W),
                         lambda b, d, j: (b, jnp.maximum(tblk(d, j) * hb - 1, 0), 0)),
            pl.BlockSpec((1, SUBLANES, W),
                         lambda b, d, j: (b, jnp.minimum((tblk(d, j) + 1) * hb, n_hb - 1), 0)),
            pl.BlockSpec((4, W), lambda b, d, j: (0, 0)),
            pl.BlockSpec((1, W), lambda b, d, j: (0, 0)),
            pl.BlockSpec((1, W, 2 * W), lambda b, d, j: (d, 0, 0)),
            pl.BlockSpec((1, 1, 2 * W), lambda b, d, j: (d, 0, 0)),
            pl.BlockSpec((1, 1, W), lambda b, d, j: (d, 0, 0)),
        ],
        out_specs=pl.BlockSpec((1, 1, tt, W), lambda b, d, j: (b, d, tblk(d, j), 0)),
        out_shape=jax.ShapeDtypeStruct((B, 2, S, W), F32),
        scratch_shapes=[
            pltpu.VMEM((tt, W), F32),
            pltpu.VMEM((tt, W), F32),
            pltpu.VMEM((1, W), F32),
        ],
        compiler_params=pltpu.CompilerParams(
            dimension_semantics=("arbitrary", "arbitrary", "arbitrary")),
        name="lru",
    )(xr, xr, xr, conv_w, conv_b.reshape(1, W), gate_w, gate_b, lru_lam)


def _block_diag(w):
    eye = jnp.eye(LRU_BLOCKS, dtype=w.dtype)
    return jnp.einsum('nde,nm->ndme', w, eye).reshape(LRU_WIDTH, LRU_WIDTH)


def _outproj_body(x_ref, attn_ref, hf_ref, hb_ref, gate_ref, wo_ref, g_ref, b_ref, o_ref):
    rec = (hf_ref[0, 0] + hb_ref[0, 0]) * _gelu(gate_ref[0])
    mix = jnp.dot(attn_ref[0], wo_ref[:ATTN_WIDTH], preferred_element_type=F32)
    mix += jnp.dot(rec.astype(BF16), wo_ref[ATTN_WIDTH:], preferred_element_type=F32)
    o_ref[0] = _layer_norm(DN_ALPHA * x_ref[0] + mix, g_ref[...], b_ref[...])


def _outproj(x, attn, h, gate, w_out, ln_g, ln_b):
    B, S, D = x.shape
    tm = _tile(S, 512)
    row = lambda b, i: (b, i, 0)
    const = lambda b, i: (0, 0)
    return pl.pallas_call(
        _outproj_body,
        grid=(B, S // tm),
        in_specs=[
            pl.BlockSpec((1, tm, D), row),
            pl.BlockSpec((1, tm, ATTN_WIDTH), row),
            pl.BlockSpec((1, 1, tm, LRU_WIDTH), lambda b, i: (b, 0, i, 0)),
            pl.BlockSpec((1, 1, tm, LRU_WIDTH), lambda b, i: (b, 1, i, 0)),
            pl.BlockSpec((1, tm, LRU_WIDTH), row),
            pl.BlockSpec((ATTN_WIDTH + LRU_WIDTH, D), const),
            pl.BlockSpec((1, D), const),
            pl.BlockSpec((1, D), const),
        ],
        out_specs=pl.BlockSpec((1, tm, D), row),
        out_shape=jax.ShapeDtypeStruct((B, S, D), F32),
        compiler_params=pltpu.CompilerParams(
            dimension_semantics=("arbitrary", "arbitrary")),
        name="outproj",
    )(x, attn, h, h, gate, w_out, ln_g.reshape(1, D), ln_b.reshape(1, D))


_CAND_PAIRS = [(i, j) for i in range(PEER_TOPK + 1) for j in range(PEER_TOPK + 1)
               if (i + 1) * (j + 1) <= PEER_TOPK + 1]
_CAND_ROWS = -(-len(_CAND_PAIRS) // SUBLANES) * SUBLANES


def _top_rows(s, n, want_rank):
    rows = []
    cur = s
    rank = jnp.full(s.shape, float(n), F32) if want_rank else None
    for r in range(n):
        mx = jnp.max(cur, axis=0, keepdims=True)
        rows.append(mx)
        hit = cur == mx
        if want_rank:
            rank = jnp.where(hit, float(r), rank)
        if r < n - 1:
            cur = jnp.where(hit, NEG_INF, cur)
    return rows, rank


def _dup_bf16(x):
    u = pltpu.bitcast(x.astype(BF16).astype(F32), jnp.uint32)
    return u | lax.shift_right_logical(u, jnp.uint32(16))


def _route_body(x_ref, wq_ref, keys_ref, cnt_ref, ga_ref, rank_ref, gb_ref, cand_sc):
    xb = x_ref[...].astype(BF16)
    qp = jnp.dot(xb, wq_ref[...], preferred_element_type=F32)
    cand_sc[...] = jnp.full(cand_sc.shape, NEG_INF, F32)
    n = PEER_TOPK + 1
    for h in range(PEER_HEADS):
        s = []
        for p in range(2):
            c0 = (2 * h + p) * PEER_HALF
            blk = qp[:, c0:c0 + PEER_HALF].astype(BF16)
            s.append(lax.dot_general(keys_ref[p], blk, _NT, preferred_element_type=F32))
        s1, s2 = s
        a, _ = _top_rows(s1, n, False)
        b, rank2 = _top_rows(s2, n, True)
        for r, (i, j) in enumerate(_CAND_PAIRS):
            cand_sc[r:r + 1, :] = a[i] + b[j]
        cand = cand_sc[...]
        cur = cand
        for r in range(PEER_TOPK):
            mx = jnp.max(cur, axis=0, keepdims=True)
            cur = jnp.where(cur == mx, NEG_INF, cur)
        thr = 0.5 * (mx + jnp.max(cur, axis=0, keepdims=True))
        z = jnp.sum(jnp.where(cand >= thr, jnp.exp(cand - (a[0] + b[0])), 0.0),
                    axis=0, keepdims=True)
        need = thr - s1
        cnt = jnp.zeros(s1.shape, F32)
        for j in range(n):
            cnt = jnp.where(b[j] >= need, j + 1.0, cnt)
        outs = (
            (cnt_ref, _dup_bf16(cnt)),
            (ga_ref, _dup_bf16(jnp.exp(s1 - a[0]) / z)),
            (rank_ref, pltpu.bitcast(rank2.astype(BF16), jnp.uint32)),
            (gb_ref, pltpu.bitcast(jnp.exp(s2 - b[0]).astype(BF16), jnp.uint32)),
        )
        for ref, val in outs:
            for c in range(val.shape[1] // LANES):
                ref[c, h] = val[:, c * LANES:(c + 1) * LANES]


def _route(x1, wq, keys):
    T, D = x1.shape
    tm = _tile(T, 256)
    shape = (T // LANES, PEER_HEADS, N_KEYS, LANES)
    packed = (T // LANES, PEER_HEADS, N_KEYS // 2, LANES)
    ospec = pl.BlockSpec((tm // LANES,) + shape[1:], lambda i: (i, 0, 0, 0))
    pspec = pl.BlockSpec((tm // LANES,) + packed[1:], lambda i: (i, 0, 0, 0))
    return pl.pallas_call(
        _route_body,
        grid=(T // tm,),
        in_specs=[
            pl.BlockSpec((tm, D), lambda i: (i, 0)),
            pl.BlockSpec((D, 2 * PEER_HEADS * PEER_HALF), lambda i: (0, 0)),
            pl.BlockSpec((2, N_KEYS, PEER_HALF), lambda i: (0, 0, 0)),
        ],
        out_specs=[ospec, ospec, pspec, pspec],
        out_shape=[jax.ShapeDtypeStruct(shape, jnp.uint32), jax.ShapeDtypeStruct(shape, jnp.uint32),
                   jax.ShapeDtypeStruct(packed, jnp.uint32), jax.ShapeDtypeStruct(packed, jnp.uint32)],
        scratch_shapes=[pltpu.VMEM((_CAND_ROWS, tm), F32)],
        compiler_params=pltpu.CompilerParams(dimension_semantics=("arbitrary",)),
        name="route",
    )(x1, wq, keys)


PEER_SUB = 1024
PEER_SUBS_PER_STEP = 4
PEER_TOK = 256
PEER_SLAB = 16


def _peer_body(x_ref, cnt_ref, ga_ref, rank_ref, gb_ref, eu_ref, evt_ref, g_ref, b_ref, o_ref,
               xb_sc, act0_sc, act1_sc, wg0_sc, wg1_sc, acc_sc):
    j = pl.program_id(1)
    n_half = xb_sc.shape[0]
    assert n_half == 2
    act_sc = (act0_sc, act1_sc)
    wg_sc = (wg0_sc, wg1_sc)

    @pl.when(j == 0)
    def _():
        for t in range(n_half):
            xb_sc[t] = x_ref[t * PEER_TOK:(t + 1) * PEER_TOK, :].astype(BF16)
        acc_sc[...] = jnp.zeros(acc_sc.shape, F32)

    def stage_a(s, t):
        eu = pltpu.bitcast(eu_ref[s], BF16)
        act_sc[t][...] = lax.dot_general(eu, xb_sc[t], _NT, preferred_element_type=F32)

    def stage_c(s, t):
        evt = pltpu.bitcast(evt_ref[s], BF16)
        acc_sc[t] += jnp.dot(evt, wg_sc[t][...], preferred_element_type=F32)

    def rows16(ref, ch, h, g):
        row = ref[ch, h, pl.ds(g, 1), :]
        return pltpu.bitcast(jnp.broadcast_to(row, (SUBLANES, LANES)), BF16)

    def stage_b(s, t):
        for gg in range(PEER_SUB // N_KEYS):
            g = s * (PEER_SUB // N_KEYS) + gg
            for lc in range(PEER_TOK // LANES):
                ch = t * (PEER_TOK // LANES) + lc
                ls = slice(lc * LANES, (lc + 1) * LANES)
                cnt = [rows16(cnt_ref, ch, h, g) for h in range(PEER_HEADS)]
                ga = [rows16(ga_ref, ch, h, g) for h in range(PEER_HEADS)]
                for sl in range(N_KEYS // PEER_SLAB):
                    rs = slice(sl * SUBLANES, (sl + 1) * SUBLANES)
                    es = slice(gg * N_KEYS + sl * PEER_SLAB, gg * N_KEYS + (sl + 1) * PEER_SLAB)
                    w = None
                    for h in range(PEER_HEADS):
                        rank = pltpu.bitcast(rank_ref[ch, h, rs, :], BF16)
                        gb = pltpu.bitcast(gb_ref[ch, h, rs, :], BF16)
                        term = jnp.where(rank < cnt[h], ga[h], 0.0) * gb
                        w = term if w is None else w + term
                    wg_sc[t][es, ls] = w * _gelu(act_sc[t][es, ls].astype(BF16))

    n_sub = PEER_SUBS_PER_STEP
    stage_a(0, 0)
    stage_a(0, 1)
    stage_b(0, 0)

    def steady(k, carry):
        stage_a(k + 1, 0)
        stage_b(k, 1)
        stage_c(k, 0)
        stage_a(k + 1, 1)
        stage_b(k + 1, 0)
        stage_c(k, 1)
        return carry

    lax.fori_loop(0, n_sub - 1, steady, 0)
    stage_b(n_sub - 1, 1)
    stage_c(n_sub - 1, 0)
    stage_c(n_sub - 1, 1)

    @pl.when(j == pl.num_programs(1) - 1)
    def _():
        for t in range(n_half):
            ts = slice(t * PEER_TOK, (t + 1) * PEER_TOK)
            y = DN_ALPHA * x_ref[ts, :] + acc_sc[t].T
            o_ref[ts, :] = _layer_norm(y, g_ref[...], b_ref[...])


def _pack_body(w_ref, o_ref, *, transpose):
    w = w_ref[...]
    if transpose:
        w = w.T
    o_ref[0] = pltpu.bitcast(w.astype(BF16), jnp.uint32)


def _pack_experts(w, transpose):
    n_exp, D = w.shape
    rows, cols = (D, PEER_SUB) if transpose else (PEER_SUB, D)
    return pl.pallas_call(
        functools.partial(_pack_body, transpose=transpose),
        grid=(n_exp // PEER_SUB,),
        in_specs=[pl.BlockSpec((PEER_SUB, D), lambda i: (i, 0))],
        out_specs=pl.BlockSpec((1, rows // 2, cols), lambda i: (i, 0, 0)),
        out_shape=jax.ShapeDtypeStruct((n_exp // PEER_SUB, rows // 2, cols), jnp.uint32),
        compiler_params=pltpu.CompilerParams(dimension_semantics=("arbitrary",)),
        name="pack",
    )(w)


def _peer(x1, cnt, ga, rank2, gb, eu3, evt3, ln_g, ln_b):
    T, D = x1.shape
    n_exp = eu3.shape[0] * PEER_SUB
    tc = _tile(T, 2 * PEER_TOK)
    assert tc == 2 * PEER_TOK
    ns = PEER_SUBS_PER_STEP
    ne = ns * PEER_SUB
    gs = ne // N_KEYS
    nch = tc // LANES
    per_group = pl.BlockSpec((nch, PEER_HEADS, gs, LANES), lambda i, j: (i, 0, j, 0))
    per_key = pl.BlockSpec((nch, PEER_HEADS, N_KEYS // 2, LANES), lambda i, j: (i, 0, 0, 0))
    const = lambda i, j: (0, 0)
    return pl.pallas_call(
        _peer_body,
        grid=(T // tc, n_exp // ne),
        in_specs=[
            pl.BlockSpec((tc, D), lambda i, j: (i, 0)),
            per_group, per_group, per_key, per_key,
            pl.BlockSpec((ns, PEER_SUB // 2, D), lambda i, j: (j, 0, 0)),
            pl.BlockSpec((ns, D // 2, PEER_SUB), lambda i, j: (j, 0, 0)),
            pl.BlockSpec((1, D), const),
            pl.BlockSpec((1, D), const),
        ],
        out_specs=pl.BlockSpec((tc, D), lambda i, j: (i, 0)),
        out_shape=jax.ShapeDtypeStruct((T, D), F32),
        scratch_shapes=[
            pltpu.VMEM((2, PEER_TOK, D), BF16),
            pltpu.VMEM((PEER_SUB, PEER_TOK), F32),
            pltpu.VMEM((PEER_SUB, PEER_TOK), F32),
            pltpu.VMEM((PEER_SUB, PEER_TOK), BF16),
            pltpu.VMEM((PEER_SUB, PEER_TOK), BF16),
            pltpu.VMEM((2, D, PEER_TOK), F32),
        ],
        compiler_params=pltpu.CompilerParams(
            dimension_semantics=("arbitrary", "arbitrary")),
        name="peer",
    )(x1, cnt, ga, rank2, gb, eu3, evt3, ln_g.reshape(1, D), ln_b.reshape(1, D))


def _trunk(x, w_in, lambda_qk, subln_g, conv_w, conv_b, gate_a_w, gate_a_b, gate_i_w,
           gate_i_b, lru_lambda, w_out, ln1_g, ln1_b, peer_wq, peer_keys, expert_u,
           expert_v, ln2_g, ln2_b):
    B, S, D = x.shape
    cos_t, sa_t, sb_t = _rotary_tables(S)
    for l in range(DEPTH):
        lambda_init = 0.8 - 0.6 * math.exp(-0.3 * l)
        q, k, v, xr, gate = _proj(x, w_in[l].astype(BF16), cos_t, sa_t, sb_t)
        attn = _attention(q, k, v, lambda_qk[l], subln_g[l], lambda_init)
        gate_w = jnp.stack([
            jnp.concatenate([_block_diag(gate_a_w[l, d]), _block_diag(gate_i_w[l, d])], axis=1)
            for d in range(2)]).astype(BF16)
        gate_b = jnp.concatenate([gate_a_b[l], gate_i_b[l]], axis=-1)[:, None, :]
        h = _lru(xr, conv_w[l], conv_b[l], gate_w, gate_b, lru_lambda[l][:, None, :])
        x1 = _outproj(x, attn, h, gate, w_out[l].astype(BF16), ln1_g[l], ln1_b[l])
        x1 = x1.reshape(B * S, D)
        cnt, ga, rank2, gb = _route(x1, peer_wq[l].astype(BF16), peer_keys[l].astype(BF16))
        x = _peer(x1, cnt, ga, rank2, gb, _pack_experts(expert_u[l], False),
                  _pack_experts(expert_v[l], True), ln2_g[l], ln2_b[l]).reshape(B, S, D)
    return x


def kernel(x_prompt, x_sample, w_in, lambda_qk, subln_g, conv_w, conv_b, gate_a_w, gate_a_b,
           gate_i_w, gate_i_b, lru_lambda, w_out, ln1_g, ln1_b, peer_wq, peer_keys, expert_u,
           expert_v, ln2_g, ln2_b):
    assert x_prompt.shape[1:] == x_sample.shape[1:]
    nb = x_prompt.shape[0]
    x = jnp.concatenate([x_prompt, x_sample], axis=0)
    y = _trunk(x, w_in, lambda_qk, subln_g, conv_w, conv_b, gate_a_w, gate_a_b, gate_i_w,
               gate_i_b, lru_lambda, w_out, ln1_g, ln1_b, peer_wq, peer_keys, expert_u,
               expert_v, ln2_g, ln2_b)
    return (y[:nb], y[nb:])
```

```python
import functools
import math

import jax
import jax.numpy as jnp
from jax import lax
from jax.experimental import pallas as pl
from jax.experimental.pallas import tpu as pltpu

F32 = jnp.float32
BF16 = jnp.bfloat16

D_MODEL = 1024
DEPTH = 2
ATTN_WIDTH = 512
LRU_WIDTH = 512
N_DIFF_HEADS = 4
QK_HEAD_DIM = 64
V_HEAD_DIM = 128
ROT_DIM = 16
ROPE_THETA = 500000.0
CONV_LEFT = 2
LRU_BLOCKS = 8
LRU_BLOCK_DIM = 64
LRU_C = 8.0
PEER_HEADS = 8
N_KEYS = 128
PEER_TOPK = 16
PEER_HALF = 128
IN_WIDTH = 3 * ATTN_WIDTH + 2 * LRU_WIDTH
DN_ALPHA = (2 * DEPTH) ** 0.25
LN_EPS = 1e-5
SUBLN_EPS = 1e-5

LANES = 128
SUBLANES = 8
NEG_INF = float("-inf")
LOG2_E = math.log2(math.e)

_NT = (((1,), (1,)), ((), ()))


def _tile(n, want):
    t = min(n, want)
    assert n % t == 0, (n, t)
    return t


def _layer_norm(y, g, b):
    mu = jnp.mean(y, axis=-1, keepdims=True)
    yc = y - mu
    var = jnp.mean(yc * yc, axis=-1, keepdims=True)
    return yc * lax.rsqrt(var + LN_EPS) * g + b


def _gelu(x):
    c = math.sqrt(2.0 / math.pi)
    hx = 0.5 * x
    return hx + hx * jnp.tanh(x * (c + (c * 0.044715) * (x * x)))


def _proj_body(x_ref, w_ref, cos_ref, sa_ref, sb_ref, q_ref, k_ref, v_ref, xr_ref, gate_ref):
    xb = x_ref[0].astype(BF16)
    cos = cos_ref[...]
    sa = sa_ref[...]
    sb = sb_ref[...]

    def rotate(t):
        return t * cos + pltpu.roll(t, 8, 1) * sa + pltpu.roll(t, LANES - 8, 1) * sb

    def proj(c0, width):
        return jnp.dot(xb, w_ref[:, c0:c0 + width], preferred_element_type=F32)

    q = proj(0, ATTN_WIDTH)
    k = proj(ATTN_WIDTH, ATTN_WIDTH)
    for h in range(N_DIFF_HEADS):
        cs = slice(h * LANES, (h + 1) * LANES)
        q_ref[0, :, cs] = (rotate(q[:, cs]) * (QK_HEAD_DIM ** -0.5 * LOG2_E)).astype(BF16)
        k_ref[0, :, cs] = rotate(k[:, cs]).astype(BF16)
    v_ref[0] = proj(2 * ATTN_WIDTH, ATTN_WIDTH).astype(BF16)
    xr_ref[0] = proj(3 * ATTN_WIDTH, LRU_WIDTH)
    gate_ref[0] = proj(3 * ATTN_WIDTH + LRU_WIDTH, LRU_WIDTH)


def _proj(x, w_in, cos_t, sa_t, sb_t):
    B, S, D = x.shape
    tm = _tile(S, 512)
    row = lambda b, i: (b, i, 0)
    tab = lambda b, i: (i, 0)
    out_bf = jax.ShapeDtypeStruct((B, S, ATTN_WIDTH), BF16)
    out_f = jax.ShapeDtypeStruct((B, S, LRU_WIDTH), F32)
    return pl.pallas_call(
        _proj_body,
        grid=(B, S // tm),
        in_specs=[
            pl.BlockSpec((1, tm, D), row),
            pl.BlockSpec((D, IN_WIDTH), lambda b, i: (0, 0)),
            pl.BlockSpec((tm, LANES), tab),
            pl.BlockSpec((tm, LANES), tab),
            pl.BlockSpec((tm, LANES), tab),
        ],
        out_specs=[pl.BlockSpec((1, tm, ATTN_WIDTH), row)] * 5,
        out_shape=[out_bf, out_bf, out_bf, out_f, out_f],
        compiler_params=pltpu.CompilerParams(
            dimension_semantics=("arbitrary", "arbitrary")),
        name="proj",
    )(x, w_in, cos_t, sa_t, sb_t)


def _rotary_tables(S):
    half = ROT_DIM // 2
    inv = ROPE_THETA ** (-jnp.arange(0, ROT_DIM, 2, dtype=F32) / ROT_DIM)
    ang = jnp.arange(S, dtype=F32)[:, None] * inv[None, :]
    cos, sin = jnp.cos(ang), jnp.sin(ang)
    ones = jnp.ones((S, QK_HEAD_DIM - ROT_DIM), F32)
    zeros = jnp.zeros((S, QK_HEAD_DIM - ROT_DIM), F32)
    z8 = jnp.zeros((S, half), F32)
    cos_t = jnp.concatenate([cos, cos, ones], axis=-1)
    sa_t = jnp.concatenate([z8, sin, zeros], axis=-1)
    sb_t = jnp.concatenate([-sin, z8, zeros], axis=-1)
    rep = lambda t: jnp.concatenate([t, t], axis=-1)
    return rep(cos_t), rep(sa_t), rep(sb_t)


def _attn_body(lam_ref, g_ref, q_ref, k_ref, v_ref, o_ref, km_sc, va_sc, m_sc, acc_sc, *,
               lambda_init, tk):
    seq = k_ref.shape[1]

    @pl.when(pl.program_id(2) == 0)
    def _():
        k = k_ref[0]
        lane = lax.broadcasted_iota(jnp.int32, k.shape, 1)
        zero = jnp.zeros_like(k)
        km_sc[0] = jnp.where(lane < QK_HEAD_DIM, k, zero)
        km_sc[1] = jnp.where(lane >= QK_HEAD_DIM, k, zero)
        va_sc[:, :V_HEAD_DIM] = v_ref[0]
        va_sc[:, V_HEAD_DIM:] = jnp.ones((seq, V_HEAD_DIM), BF16)

    m_sc[...] = jnp.full(m_sc.shape, NEG_INF, F32)
    acc_sc[...] = jnp.zeros(acc_sc.shape, F32)
    q = q_ref[0]

    def chunk(j, carry):
        ks = pl.ds(pl.multiple_of(j * tk, tk), tk)
        for c in range(2):
            s = lax.dot_general(q, km_sc[c, ks, :], _NT, preferred_element_type=F32)
            m_prev = m_sc[c]
            m_new = jnp.maximum(m_prev, jnp.max(s, axis=1, keepdims=True))
            alpha = jnp.exp2(m_prev - m_new)
            p = jnp.exp2(s - jnp.tile(m_new, (1, tk // LANES)))
            pv = jnp.dot(p.astype(BF16), va_sc[ks, :], preferred_element_type=F32)
            acc_sc[c] = jnp.tile(alpha, (1, 2)) * acc_sc[c] + pv
            m_sc[c] = m_new
        return carry

    lax.fori_loop(0, seq // tk, chunk, 0, unroll=True)

    lp = lam_ref[...]
    d1 = jnp.sum(lp[0:1] * lp[1:2], axis=1, keepdims=True)
    d2 = jnp.sum(lp[2:3] * lp[3:4], axis=1, keepdims=True)
    lam = jnp.exp(d1) - jnp.exp(d2) + lambda_init
    acc1 = acc_sc[0]
    acc2 = acc_sc[1]
    o = (acc1[:, :V_HEAD_DIM] / acc1[:, V_HEAD_DIM:]
         - lam * (acc2[:, :V_HEAD_DIM] / acc2[:, V_HEAD_DIM:]))
    o = o * lax.rsqrt(jnp.mean(o * o, axis=1, keepdims=True) + SUBLN_EPS) * g_ref[...]
    o_ref[0] = (o * (1.0 - lambda_init)).astype(o_ref.dtype)


def _attention(q, k, v, lambda_qk, subln_g, lambda_init):
    B, S, _ = q.shape
    tq = _tile(S, 1024)
    tk = _tile(S, 1024)
    return pl.pallas_call(
        functools.partial(_attn_body, lambda_init=lambda_init, tk=tk),
        grid=(B, N_DIFF_HEADS, S // tq),
        in_specs=[
            pl.BlockSpec((4, QK_HEAD_DIM), lambda b, h, i: (0, 0)),
            pl.BlockSpec((1, V_HEAD_DIM), lambda b, h, i: (0, 0)),
            pl.BlockSpec((1, tq, LANES), lambda b, h, i: (b, i, h)),
            pl.BlockSpec((1, S, LANES), lambda b, h, i: (b, 0, h)),
            pl.BlockSpec((1, S, LANES), lambda b, h, i: (b, 0, h)),
        ],
        out_specs=pl.BlockSpec((1, tq, LANES), lambda b, h, i: (b, i, h)),
        out_shape=jax.ShapeDtypeStruct((B, S, ATTN_WIDTH), BF16),
        scratch_shapes=[
            pltpu.VMEM((2, S, LANES), BF16),
            pltpu.VMEM((S, 2 * V_HEAD_DIM), BF16),
            pltpu.VMEM((2, tq, LANES), F32),
            pltpu.VMEM((2, tq, 2 * V_HEAD_DIM), F32),
        ],
        compiler_params=pltpu.CompilerParams(
            dimension_semantics=("arbitrary", "arbitrary", "arbitrary")),
        name="attn",
    )(lambda_qk, subln_g.reshape(1, V_HEAD_DIM), q, k, v)


def _lru_body(x_ref, prev_ref, next_ref, cw_ref, cb_ref, gw_ref, gb_ref, lam_ref, h_ref,
              a_sc, b_sc, carry_sc):
    d = pl.program_id(1)
    j = pl.program_id(2)
    n_t = pl.num_programs(2)
    tt = x_ref.shape[1]
    jj = jnp.where(d == 0, j, n_t - 1 - j)

    @pl.when(j == 0)
    def _():
        carry_sc[...] = jnp.zeros(carry_sc.shape, F32)

    x = x_ref[0]
    prev = prev_ref[0] * (jj > 0).astype(F32)
    nxt = next_ref[0] * (jj < n_t - 1).astype(F32)
    row = lax.broadcasted_iota(jnp.int32, x.shape, 0)
    cw = cw_ref[...]
    x_m1 = jnp.where(row == 0, prev[7:8], pltpu.roll(x, 1, 0))
    x_m2 = jnp.where(row == 0, prev[6:7], jnp.where(row == 1, prev[7:8], pltpu.roll(x, 2, 0)))
    x_p1 = jnp.where(row == tt - 1, nxt[0:1], pltpu.roll(x, tt - 1, 0))
    xc = x_m2 * cw[0:1] + x_m1 * cw[1:2] + x * cw[2:3] + x_p1 * cw[3:4] + cb_ref[...]

    gates = jnp.dot(xc.astype(BF16), gw_ref[0], preferred_element_type=F32) + gb_ref[0]
    r = jax.nn.sigmoid(gates[:, :LRU_WIDTH])
    i = jax.nn.sigmoid(gates[:, LRU_WIDTH:])
    lam = lam_ref[0]
    softplus_neg = jnp.maximum(-lam, 0.0) + jnp.log(1.0 + jnp.exp(-jnp.abs(lam)))
    log_a = -LRU_C * r * softplus_neg
    a = jnp.exp(log_a)
    b = jnp.sqrt(1.0 - a * a) * (i * xc)

    sub = row & (SUBLANES - 1)

    def local_scan(a, b, reverse):
        shift = 1
        while shift < SUBLANES:
            if reverse:
                a_s = pltpu.roll(a, tt - shift, 0)
                b_s = pltpu.roll(b, tt - shift, 0)
                valid = sub < SUBLANES - shift
            else:
                a_s = pltpu.roll(a, shift, 0)
                b_s = pltpu.roll(b, shift, 0)
                valid = sub >= shift
            b = a * jnp.where(valid, b_s, 0.0) + b
            a = a * jnp.where(valid, a_s, 1.0)
            shift *= 2
        return a, b

    n_groups = tt // SUBLANES

    def sweep(reverse):
        a_l, b_l = local_scan(a, b, reverse)
        a_sc[...] = a_l
        b_sc[...] = b_l

        def step(g, carry):
            gi = (n_groups - 1 - g) if reverse else g
            rs = pl.ds(pl.multiple_of(gi * SUBLANES, SUBLANES), SUBLANES)
            h = b_sc[rs, :] + a_sc[rs, :] * carry
            h_ref[0, 0, rs, :] = h
            return h[0:1] if reverse else h[SUBLANES - 1:SUBLANES]

        carry_sc[...] = lax.fori_loop(0, n_groups, step, carry_sc[...])

    @pl.when(d == 0)
    def _():
        sweep(False)

    @pl.when(d == 1)
    def _():
        sweep(True)


def _lru(xr, conv_w, conv_b, gate_w, gate_b, lru_lam):
    B, S, W = xr.shape
    tt = _tile(S, 1024)
    n_t = S // tt
    hb = tt // SUBLANES
    n_hb = S // SUBLANES

    def tblk(d, j):
        return jnp.where(d == 0, j, n_t - 1 - j)

    return pl.pallas_call(
        _lru_body,
        grid=(B, 2, n_t),
        in_specs=[
            pl.BlockSpec((1, tt, W), lambda b, d, j: (b, tblk(d, j), 0)),
            pl.BlockSpec((1, SUBLANES, W),
                         lambda b, d, j: (b, jnp.maximum(tblk(d, j) * hb - 1, 0), 0)),
            pl.BlockSpec((1, SUBLANES, W),
                         lambda b, d, j: (b, jnp.minimum((tblk(d, j) + 1) * hb, n_hb - 1), 0)),
            pl.BlockSpec((4, W), lambda b, d, j: (0, 0)),
            pl.BlockSpec((1, W), lambda b, d, j: (0, 0)),
            pl.BlockSpec((1, W, 2 * W), lambda b, d, j: (d, 0, 0)),
            pl.BlockSpec((1, 1, 2 * W), lambda b, d, j: (d, 0, 0)),
            pl.BlockSpec((1, 1, W), lambda b, d, j: (d, 0, 0)),
        ],
        out_specs=pl.BlockSpec((1, 1, tt, W), lambda b, d, j: (b, d, tblk(d, j), 0)),
        out_shape=jax.ShapeDtypeStruct((B, 2, S, W), F32),
        scratch_shapes=[
            pltpu.VMEM((tt, W), F32),
            pltpu.VMEM((tt, W), F32),
            pltpu.VMEM((1, W), F32),
        ],
        compiler_params=pltpu.CompilerParams(
            dimension_semantics=("arbitrary", "arbitrary", "arbitrary")),
        name="lru",
    )(xr, xr, xr, conv_w, conv_b.reshape(1, W), gate_w, gate_b, lru_lam)


def _block_diag(w):
    eye = jnp.eye(LRU_BLOCKS, dtype=w.dtype)
    return jnp.einsum('nde,nm->ndme', w, eye).reshape(LRU_WIDTH, LRU_WIDTH)


def _outproj_body(x_ref, attn_ref, hf_ref, hb_ref, gate_ref, wo_ref, g_ref, b_ref, o_ref):
    rec = (hf_ref[0, 0] + hb_ref[0, 0]) * _gelu(gate_ref[0])
    mix = jnp.dot(attn_ref[0], wo_ref[:ATTN_WIDTH], preferred_element_type=F32)
    mix += jnp.dot(rec.astype(BF16), wo_ref[ATTN_WIDTH:], preferred_element_type=F32)
    o_ref[0] = _layer_norm(DN_ALPHA * x_ref[0] + mix, g_ref[...], b_ref[...])


def _outproj(x, attn, h, gate, w_out, ln_g, ln_b):
    B, S, D = x.shape
    tm = _tile(S, 512)
    row = lambda b, i: (b, i, 0)
    const = lambda b, i: (0, 0)
    return pl.pallas_call(
        _outproj_body,
        grid=(B, S // tm),
        in_specs=[
            pl.BlockSpec((1, tm, D), row),
            pl.BlockSpec((1, tm, ATTN_WIDTH), row),
            pl.BlockSpec((1, 1, tm, LRU_WIDTH), lambda b, i: (b, 0, i, 0)),
            pl.BlockSpec((1, 1, tm, LRU_WIDTH), lambda b, i: (b, 1, i, 0)),
            pl.BlockSpec((1, tm, LRU_WIDTH), row),
            pl.BlockSpec((ATTN_WIDTH + LRU_WIDTH, D), const),
            pl.BlockSpec((1, D), const),
            pl.BlockSpec((1, D), const),
        ],
        out_specs=pl.BlockSpec((1, tm, D), row),
        out_shape=jax.ShapeDtypeStruct((B, S, D), F32),
        compiler_params=pltpu.CompilerParams(
            dimension_semantics=("arbitrary", "arbitrary")),
        name="outproj",
    )(x, attn, h, h, gate, w_out, ln_g.reshape(1, D), ln_b.reshape(1, D))


_CAND_PAIRS = [(i, j) for i in range(PEER_TOPK + 1) for j in range(PEER_TOPK + 1)
               if (i + 1) * (j + 1) <= PEER_TOPK + 1]
_CAND_ROWS = -(-len(_CAND_PAIRS) // SUBLANES) * SUBLANES


def _batcher_pairs(n):
    def merge(lo, hi, r):
        step = 2 * r
        if step < hi - lo:
            yield from merge(lo, hi, step)
            yield from merge(lo + r, hi, step)
            yield from ((i, i + r) for i in range(lo + r, hi - r, step))
        else:
            yield (lo, lo + r)

    def sort(lo, hi):
        if hi - lo >= 1:
            mid = lo + (hi - lo) // 2
            yield from sort(lo, mid)
            yield from sort(mid + 1, hi)
            yield from merge(lo, hi, 1)

    return list(sort(0, n - 1))


def _top_values(s, n):
    v = [s[i * SUBLANES:(i + 1) * SUBLANES, :] for i in range(s.shape[0] // SUBLANES)]
    for i, j in _batcher_pairs(len(v)):
        v[i], v[j] = jnp.maximum(v[i], v[j]), jnp.minimum(v[i], v[j])
    out = []
    for r in range(n):
        mx = jnp.max(v[0], axis=0, keepdims=True)
        out.append(mx)
        depth = min(len(v), n - 1 - r)
        if depth:
            hit = v[0] == mx
            for i in range(depth):
                below = v[i + 1] if i + 1 < len(v) else NEG_INF
                v[i] = jnp.where(hit, below, v[i])
    return out


def _dup_bf16(x):
    u = pltpu.bitcast(x.astype(BF16).astype(F32), jnp.uint32)
    return u | lax.shift_right_logical(u, jnp.uint32(16))


def _route_body(x_ref, wq_ref, keys_ref, cnt_ref, ga_ref, rank_ref, gb_ref, cand_sc):
    xb = x_ref[...].astype(BF16)
    qp = jnp.dot(xb, wq_ref[...], preferred_element_type=F32)
    cand_sc[...] = jnp.full(cand_sc.shape, NEG_INF, F32)
    n = PEER_TOPK + 1
    for h in range(PEER_HEADS):
        s = []
        for p in range(2):
            c0 = (2 * h + p) * PEER_HALF
            blk = qp[:, c0:c0 + PEER_HALF].astype(BF16)
            s.append(lax.dot_general(keys_ref[p], blk, _NT, preferred_element_type=F32))
        s1, s2 = s
        a = _top_values(s1, n)
        b = _top_values(s2, n)
        rank2 = jnp.full(s2.shape, float(n), F32)
        for j in reversed(range(n)):
            rank2 = jnp.where(s2 >= b[j], float(j), rank2)
        for r, (i, j) in enumerate(_CAND_PAIRS):
            cand_sc[r:r + 1, :] = a[i] + b[j]
        cand = cand_sc[...]
        cur = cand
        for r in range(PEER_TOPK):
            mx = jnp.max(cur, axis=0, keepdims=True)
            cur = jnp.where(cur == mx, NEG_INF, cur)
        thr = 0.5 * (mx + jnp.max(cur, axis=0, keepdims=True))
        z = jnp.sum(jnp.where(cand >= thr, jnp.exp(cand - (a[0] + b[0])), 0.0),
                    axis=0, keepdims=True)
        need = thr - s1
        cnt = jnp.zeros(s1.shape, F32)
        for j in range(n):
            cnt = jnp.where(b[j] >= need, j + 1.0, cnt)
        outs = (
            (cnt_ref, _dup_bf16(cnt)),
            (ga_ref, _dup_bf16(jnp.exp(s1 - a[0]) / z)),
            (rank_ref, pltpu.bitcast(rank2.astype(BF16), jnp.uint32)),
            (gb_ref, pltpu.bitcast(jnp.exp(s2 - b[0]).astype(BF16), jnp.uint32)),
        )
        for ref, val in outs:
            for c in range(val.shape[1] // LANES):
                ref[c, h] = val[:, c * LANES:(c + 1) * LANES]


def _route(x1, wq, keys):
    T, D = x1.shape
    tm = _tile(T, 256)
    shape = (T // LANES, PEER_HEADS, N_KEYS, LANES)
    packed = (T // LANES, PEER_HEADS, N_KEYS // 2, LANES)
    ospec = pl.BlockSpec((tm // LANES,) + shape[1:], lambda i: (i, 0, 0, 0))
    pspec = pl.BlockSpec((tm // LANES,) + packed[1:], lambda i: (i, 0, 0, 0))
    return pl.pallas_call(
        _route_body,
        grid=(T // tm,),
        in_specs=[
            pl.BlockSpec((tm, D), lambda i: (i, 0)),
            pl.BlockSpec((D, 2 * PEER_HEADS * PEER_HALF), lambda i: (0, 0)),
            pl.BlockSpec((2, N_KEYS, PEER_HALF), lambda i: (0, 0, 0)),
        ],
        out_specs=[ospec, ospec, pspec, pspec],
        out_shape=[jax.ShapeDtypeStruct(shape, jnp.uint32), jax.ShapeDtypeStruct(shape, jnp.uint32),
                   jax.ShapeDtypeStruct(packed, jnp.uint32), jax.ShapeDtypeStruct(packed, jnp.uint32)],
        scratch_shapes=[pltpu.VMEM((_CAND_ROWS, tm), F32)],
        compiler_params=pltpu.CompilerParams(dimension_semantics=("arbitrary",)),
        name="route",
    )(x1, wq, keys)


PEER_SUB = 1024
PEER_SUBS_PER_STEP = 2
PEER_TOK = 256
PEER_TOK_PARTS = 4
PEER_SLAB = 16


def _peer_body(x_ref, cnt_ref, ga_ref, rank_ref, gb_ref, eu_ref, evt_ref, g_ref, b_ref, o_ref,
               xb_sc, act0_sc, act1_sc, wg0_sc, wg1_sc, acc_sc):
    j = pl.program_id(1)
    n_part = xb_sc.shape[0]
    n_stage = PEER_SUBS_PER_STEP * n_part
    assert n_part % 2 == 0 and n_part & (n_part - 1) == 0
    part_bits = n_part.bit_length() - 1
    act_sc = (act0_sc, act1_sc)
    wg_sc = (wg0_sc, wg1_sc)

    @pl.when(j == 0)
    def _():
        for t in range(n_part):
            xb_sc[t] = x_ref[t * PEER_TOK:(t + 1) * PEER_TOK, :].astype(BF16)
        acc_sc[...] = jnp.zeros(acc_sc.shape, F32)

    def split(i):
        if isinstance(i, int):
            return i // n_part, i % n_part
        return lax.shift_right_logical(i, part_bits), i & (n_part - 1)

    def stage_a(i, p):
        s, t = split(i)
        eu = pltpu.bitcast(eu_ref[s], BF16)
        act_sc[p][...] = lax.dot_general(eu, xb_sc[t], _NT, preferred_element_type=F32)

    def stage_c(i, p):
        s, t = split(i)
        evt = pltpu.bitcast(evt_ref[s], BF16)
        acc_sc[t] += jnp.dot(evt, wg_sc[p][...], preferred_element_type=F32)

    def rows16(ref, ch, h, g):
        row = ref[ch, h, pl.ds(g, 1), :]
        return pltpu.bitcast(jnp.broadcast_to(row, (SUBLANES, LANES)), BF16)

    def stage_b(i, p):
        s, t = split(i)
        for gg in range(PEER_SUB // N_KEYS):
            g = s * (PEER_SUB // N_KEYS) + gg
            for lc in range(PEER_TOK // LANES):
                ch = t * (PEER_TOK // LANES) + lc
                ls = slice(lc * LANES, (lc + 1) * LANES)
                cnt = [rows16(cnt_ref, ch, h, g) for h in range(PEER_HEADS)]
                ga = [rows16(ga_ref, ch, h, g) for h in range(PEER_HEADS)]
                for sl in range(N_KEYS // PEER_SLAB):
                    rs = slice(sl * SUBLANES, (sl + 1) * SUBLANES)
                    es = slice(gg * N_KEYS + sl * PEER_SLAB, gg * N_KEYS + (sl + 1) * PEER_SLAB)
                    w = None
                    for h in range(PEER_HEADS):
                        rank = pltpu.bitcast(rank_ref[ch, h, rs, :], BF16)
                        gb = pltpu.bitcast(gb_ref[ch, h, rs, :], BF16)
                        term = jnp.where(rank < cnt[h], ga[h], 0.0) * gb
                        w = term if w is None else w + term
                    wg_sc[p][es, ls] = w * _gelu(act_sc[p][es, ls].astype(BF16))

    stage_a(0, 0)
    stage_a(1, 1)
    stage_b(0, 0)

    def steady(k, carry):
        i = 2 * k
        stage_a(i + 2, 0)
        stage_b(i + 1, 1)
        stage_c(i, 0)
        stage_a(i + 3, 1)
        stage_b(i + 2, 0)
        stage_c(i + 1, 1)
        return carry

    lax.fori_loop(0, n_stage // 2 - 1, steady, 0)
    stage_b(n_stage - 1, 1)
    stage_c(n_stage - 2, 0)
    stage_c(n_stage - 1, 1)

    @pl.when(j == pl.num_programs(1) - 1)
    def _():
        for t in range(n_part):
            ts = slice(t * PEER_TOK, (t + 1) * PEER_TOK)
            y = DN_ALPHA * x_ref[ts, :] + acc_sc[t].T
            o_ref[ts, :] = _layer_norm(y, g_ref[...], b_ref[...])


def _pack_body(w_ref, o_ref, *, transpose):
    w = w_ref[...]
    if transpose:
        w = w.T
    o_ref[0] = pltpu.bitcast(w.astype(BF16), jnp.uint32)


def _pack_experts(w, transpose):
    n_exp, D = w.shape
    rows, cols = (D, PEER_SUB) if transpose else (PEER_SUB, D)
    return pl.pallas_call(
        functools.partial(_pack_body, transpose=transpose),
        grid=(n_exp // PEER_SUB,),
        in_specs=[pl.BlockSpec((PEER_SUB, D), lambda i: (i, 0))],
        out_specs=pl.BlockSpec((1, rows // 2, cols), lambda i: (i, 0, 0)),
        out_shape=jax.ShapeDtypeStruct((n_exp // PEER_SUB, rows // 2, cols), jnp.uint32),
        compiler_params=pltpu.CompilerParams(dimension_semantics=("arbitrary",)),
        name="pack",
    )(w)


def _peer(x1, cnt, ga, rank2, gb, eu3, evt3, ln_g, ln_b):
    T, D = x1.shape
    n_exp = eu3.shape[0] * PEER_SUB
    tc = PEER_TOK_PARTS * PEER_TOK
    if T % tc:
        tc = 2 * PEER_TOK
    assert T % tc == 0
    ns = PEER_SUBS_PER_STEP
    ne = ns * PEER_SUB
    gs = ne // N_KEYS
    nch = tc // LANES
    per_group = pl.BlockSpec((nch, PEER_HEADS, gs, LANES), lambda i, j: (i, 0, j, 0))
    per_key = pl.BlockSpec((nch, PEER_HEADS, N_KEYS // 2, LANES), lambda i, j: (i, 0, 0, 0))
    const = lambda i, j: (0, 0)
    return pl.pallas_call(
        _peer_body,
        grid=(T // tc, n_exp // ne),
        in_specs=[
            pl.BlockSpec((tc, D), lambda i, j: (i, 0)),
            per_group, per_group, per_key, per_key,
            pl.BlockSpec((ns, PEER_SUB // 2, D), lambda i, j: (j, 0, 0)),
            pl.BlockSpec((ns, D // 2, PEER_SUB), lambda i, j: (j, 0, 0)),
            pl.BlockSpec((1, D), const),
            pl.BlockSpec((1, D), const),
        ],
        out_specs=pl.BlockSpec((tc, D), lambda i, j: (i, 0)),
        out_shape=jax.ShapeDtypeStruct((T, D), F32),
        scratch_shapes=[
            pltpu.VMEM((tc // PEER_TOK, PEER_TOK, D), BF16),
            pltpu.VMEM((PEER_SUB, PEER_TOK), F32),
            pltpu.VMEM((PEER_SUB, PEER_TOK), F32),
            pltpu.VMEM((PEER_SUB, PEER_TOK), BF16),
            pltpu.VMEM((PEER_SUB, PEER_TOK), BF16),
            pltpu.VMEM((tc // PEER_TOK, D, PEER_TOK), F32),
        ],
        compiler_params=pltpu.CompilerParams(
            dimension_semantics=("arbitrary", "arbitrary")),
        name="peer",
    )(x1, cnt, ga, rank2, gb, eu3, evt3, ln_g.reshape(1, D), ln_b.reshape(1, D))


def _prepare_layer(l, w_in, lambda_qk, subln_g, conv_w, conv_b, gate_a_w, gate_a_b, gate_i_w,
                   gate_i_b, lru_lambda, w_out, ln1_g, ln1_b, peer_wq, peer_keys, expert_u,
                   expert_v, ln2_g, ln2_b):
    gate_w = jnp.stack([
        jnp.concatenate([_block_diag(gate_a_w[l, d]), _block_diag(gate_i_w[l, d])], axis=1)
        for d in range(2)]).astype(BF16)
    return dict(
        lambda_init=0.8 - 0.6 * math.exp(-0.3 * l),
        w_in=w_in[l].astype(BF16), lambda_qk=lambda_qk[l], subln_g=subln_g[l],
        conv_w=conv_w[l], conv_b=conv_b[l], gate_w=gate_w,
        gate_b=jnp.concatenate([gate_a_b[l], gate_i_b[l]], axis=-1)[:, None, :],
        lru_lambda=lru_lambda[l][:, None, :], w_out=w_out[l].astype(BF16),
        ln1_g=ln1_g[l], ln1_b=ln1_b[l], peer_wq=peer_wq[l].astype(BF16),
        peer_keys=peer_keys[l].astype(BF16), eu=_pack_experts(expert_u[l], False),
        evt=_pack_experts(expert_v[l], True), ln2_g=ln2_g[l], ln2_b=ln2_b[l])


def _layer(x, p, tables):
    B, S, D = x.shape
    q, k, v, xr, gate = _proj(x, p["w_in"], *tables)
    attn = _attention(q, k, v, p["lambda_qk"], p["subln_g"], p["lambda_init"])
    h = _lru(xr, p["conv_w"], p["conv_b"], p["gate_w"], p["gate_b"], p["lru_lambda"])
    x1 = _outproj(x, attn, h, gate, p["w_out"], p["ln1_g"], p["ln1_b"]).reshape(B * S, D)
    cnt, ga, rank2, gb = _route(x1, p["peer_wq"], p["peer_keys"])
    return _peer(x1, cnt, ga, rank2, gb, p["eu"], p["evt"], p["ln2_g"], p["ln2_b"]).reshape(B, S, D)


def _trunks(xs, *weights):
    layers = [_prepare_layer(l, *weights) for l in range(DEPTH)]
    outs = []
    for x in xs:
        tables = _rotary_tables(x.shape[1])
        for p in layers:
            x = _layer(x, p, tables)
        outs.append(x)
    return tuple(outs)


def _trunk(x, *weights):
    return _trunks((x,), *weights)[0]


def kernel(x_prompt, x_sample, w_in, lambda_qk, subln_g, conv_w, conv_b, gate_a_w, gate_a_b,
           gate_i_w, gate_i_b, lru_lambda, w_out, ln1_g, ln1_b, peer_wq, peer_keys, expert_u,
           expert_v, ln2_g, ln2_b):
    return _trunks((x_prompt, x_sample), w_in, lambda_qk, subln_g, conv_w, conv_b, gate_a_w,
                   gate_a_b, gate_i_w, gate_i_b, lru_lambda, w_out, ln1_g, ln1_b, peer_wq,
                   peer_keys, expert_u, expert_v, ln2_g, ln2_b)
```

```python
import functools
import math

import jax
import jax.numpy as jnp
from jax import lax
from jax.experimental import pallas as pl
from jax.experimental.pallas import tpu as pltpu

F32 = jnp.float32
BF16 = jnp.bfloat16

D_MODEL = 1024
DEPTH = 2
ATTN_WIDTH = 512
LRU_WIDTH = 512
N_DIFF_HEADS = 4
QK_HEAD_DIM = 64
V_HEAD_DIM = 128
ROT_DIM = 16
ROPE_THETA = 500000.0
CONV_LEFT = 2
LRU_BLOCKS = 8
LRU_BLOCK_DIM = 64
LRU_C = 8.0
PEER_HEADS = 8
N_KEYS = 128
PEER_TOPK = 16
PEER_HALF = 128
IN_WIDTH = 3 * ATTN_WIDTH + 2 * LRU_WIDTH
DN_ALPHA = (2 * DEPTH) ** 0.25
LN_EPS = 1e-5
SUBLN_EPS = 1e-5

LANES = 128
SUBLANES = 8
NEG_INF = float("-inf")
LOG2_E = math.log2(math.e)

_NT = (((1,), (1,)), ((), ()))


def _tile(n, want):
    t = min(n, want)
    assert n % t == 0, (n, t)
    return t


def _layer_norm(y, g, b):
    mu = jnp.mean(y, axis=-1, keepdims=True)
    yc = y - mu
    var = jnp.mean(yc * yc, axis=-1, keepdims=True)
    return yc * lax.rsqrt(var + LN_EPS) * g + b


def _gelu(x):
    c = math.sqrt(2.0 / math.pi)
    hx = 0.5 * x
    return hx + hx * jnp.tanh(x * (c + (c * 0.044715) * (x * x)))


def _proj_body(x_ref, w_ref, cos_ref, sa_ref, sb_ref, q_ref, k_ref, v_ref, xr_ref, gate_ref):
    xb = x_ref[0].astype(BF16)
    cos = cos_ref[...]
    sa = sa_ref[...]
    sb = sb_ref[...]

    def rotate(t):
        return t * cos + pltpu.roll(t, 8, 1) * sa + pltpu.roll(t, LANES - 8, 1) * sb

    def proj(c0, width):
        return jnp.dot(xb, w_ref[:, c0:c0 + width], preferred_element_type=F32)

    q = proj(0, ATTN_WIDTH)
    k = proj(ATTN_WIDTH, ATTN_WIDTH)
    for h in range(N_DIFF_HEADS):
        cs = slice(h * LANES, (h + 1) * LANES)
        q_ref[0, :, cs] = (rotate(q[:, cs]) * (QK_HEAD_DIM ** -0.5 * LOG2_E)).astype(BF16)
        k_ref[0, :, cs] = rotate(k[:, cs]).astype(BF16)
    v_ref[0] = proj(2 * ATTN_WIDTH, ATTN_WIDTH).astype(BF16)
    xr_ref[0] = proj(3 * ATTN_WIDTH, LRU_WIDTH)
    gate_ref[0] = proj(3 * ATTN_WIDTH + LRU_WIDTH, LRU_WIDTH)


def _proj(x, w_in, cos_t, sa_t, sb_t):
    B, S, D = x.shape
    tm = _tile(S, 512)
    row = lambda b, i: (b, i, 0)
    tab = lambda b, i: (i, 0)
    out_bf = jax.ShapeDtypeStruct((B, S, ATTN_WIDTH), BF16)
    out_f = jax.ShapeDtypeStruct((B, S, LRU_WIDTH), F32)
    return pl.pallas_call(
        _proj_body,
        grid=(B, S // tm),
        in_specs=[
            pl.BlockSpec((1, tm, D), row),
            pl.BlockSpec((D, IN_WIDTH), lambda b, i: (0, 0)),
            pl.BlockSpec((tm, LANES), tab),
            pl.BlockSpec((tm, LANES), tab),
            pl.BlockSpec((tm, LANES), tab),
        ],
        out_specs=[pl.BlockSpec((1, tm, ATTN_WIDTH), row)] * 5,
        out_shape=[out_bf, out_bf, out_bf, out_f, out_f],
        compiler_params=pltpu.CompilerParams(
            dimension_semantics=("arbitrary", "arbitrary")),
        name="proj",
    )(x, w_in, cos_t, sa_t, sb_t)


def _rotary_tables(S):
    half = ROT_DIM // 2
    inv = ROPE_THETA ** (-jnp.arange(0, ROT_DIM, 2, dtype=F32) / ROT_DIM)
    ang = jnp.arange(S, dtype=F32)[:, None] * inv[None, :]
    cos, sin = jnp.cos(ang), jnp.sin(ang)
    ones = jnp.ones((S, QK_HEAD_DIM - ROT_DIM), F32)
    zeros = jnp.zeros((S, QK_HEAD_DIM - ROT_DIM), F32)
    z8 = jnp.zeros((S, half), F32)
    cos_t = jnp.concatenate([cos, cos, ones], axis=-1)
    sa_t = jnp.concatenate([z8, sin, zeros], axis=-1)
    sb_t = jnp.concatenate([-sin, z8, zeros], axis=-1)
    rep = lambda t: jnp.concatenate([t, t], axis=-1)
    return rep(cos_t), rep(sa_t), rep(sb_t)


def _attn_body(lam_ref, g_ref, q_ref, k_ref, v_ref, o_ref, km_sc, va_sc, m_sc, acc_sc, *,
               lambda_init, tk):
    seq = k_ref.shape[1]

    @pl.when(pl.program_id(2) == 0)
    def _():
        k = k_ref[0]
        lane = lax.broadcasted_iota(jnp.int32, k.shape, 1)
        zero = jnp.zeros_like(k)
        km_sc[0] = jnp.where(lane < QK_HEAD_DIM, k, zero)
        km_sc[1] = jnp.where(lane >= QK_HEAD_DIM, k, zero)
        va_sc[:, :V_HEAD_DIM] = v_ref[0]
        va_sc[:, V_HEAD_DIM:] = jnp.ones((seq, V_HEAD_DIM), BF16)

    m_sc[...] = jnp.full(m_sc.shape, NEG_INF, F32)
    acc_sc[...] = jnp.zeros(acc_sc.shape, F32)
    q = q_ref[0]

    def chunk(j, carry):
        ks = pl.ds(pl.multiple_of(j * tk, tk), tk)
        for c in range(2):
            s = lax.dot_general(q, km_sc[c, ks, :], _NT, preferred_element_type=F32)
            m_prev = m_sc[c]
            m_new = jnp.maximum(m_prev, jnp.max(s, axis=1, keepdims=True))
            alpha = jnp.exp2(m_prev - m_new)
            p = jnp.exp2(s - jnp.tile(m_new, (1, tk // LANES)))
            pv = jnp.dot(p.astype(BF16), va_sc[ks, :], preferred_element_type=F32)
            acc_sc[c] = jnp.tile(alpha, (1, 2)) * acc_sc[c] + pv
            m_sc[c] = m_new
        return carry

    lax.fori_loop(0, seq // tk, chunk, 0, unroll=True)

    lp = lam_ref[...]
    d1 = jnp.sum(lp[0:1] * lp[1:2], axis=1, keepdims=True)
    d2 = jnp.sum(lp[2:3] * lp[3:4], axis=1, keepdims=True)
    lam = jnp.exp(d1) - jnp.exp(d2) + lambda_init
    acc1 = acc_sc[0]
    acc2 = acc_sc[1]
    o = (acc1[:, :V_HEAD_DIM] / acc1[:, V_HEAD_DIM:]
         - lam * (acc2[:, :V_HEAD_DIM] / acc2[:, V_HEAD_DIM:]))
    o = o * lax.rsqrt(jnp.mean(o * o, axis=1, keepdims=True) + SUBLN_EPS) * g_ref[...]
    o_ref[0] = (o * (1.0 - lambda_init)).astype(o_ref.dtype)


def _attention(q, k, v, lambda_qk, subln_g, lambda_init):
    B, S, _ = q.shape
    tq = _tile(S, 1024)
    tk = _tile(S, 1024)
    return pl.pallas_call(
        functools.partial(_attn_body, lambda_init=lambda_init, tk=tk),
        grid=(B, N_DIFF_HEADS, S // tq),
        in_specs=[
            pl.BlockSpec((4, QK_HEAD_DIM), lambda b, h, i: (0, 0)),
            pl.BlockSpec((1, V_HEAD_DIM), lambda b, h, i: (0, 0)),
            pl.BlockSpec((1, tq, LANES), lambda b, h, i: (b, i, h)),
            pl.BlockSpec((1, S, LANES), lambda b, h, i: (b, 0, h)),
            pl.BlockSpec((1, S, LANES), lambda b, h, i: (b, 0, h)),
        ],
        out_specs=pl.BlockSpec((1, tq, LANES), lambda b, h, i: (b, i, h)),
        out_shape=jax.ShapeDtypeStruct((B, S, ATTN_WIDTH), BF16),
        scratch_shapes=[
            pltpu.VMEM((2, S, LANES), BF16),
            pltpu.VMEM((S, 2 * V_HEAD_DIM), BF16),
            pltpu.VMEM((2, tq, LANES), F32),
            pltpu.VMEM((2, tq, 2 * V_HEAD_DIM), F32),
        ],
        compiler_params=pltpu.CompilerParams(
            dimension_semantics=("arbitrary", "arbitrary", "arbitrary")),
        name="attn",
    )(lambda_qk, subln_g.reshape(1, V_HEAD_DIM), q, k, v)


def _lru_body(x_ref, prev_ref, next_ref, cw_ref, cb_ref, gw_ref, gb_ref, lam_ref, h_ref,
              a_sc, b_sc, carry_sc):
    d = pl.program_id(1)
    j = pl.program_id(2)
    n_t = pl.num_programs(2)
    tt = x_ref.shape[1]
    jj = jnp.where(d == 0, j, n_t - 1 - j)

    @pl.when(j == 0)
    def _():
        carry_sc[...] = jnp.zeros(carry_sc.shape, F32)

    x = x_ref[0]
    prev = prev_ref[0] * (jj > 0).astype(F32)
    nxt = next_ref[0] * (jj < n_t - 1).astype(F32)
    row = lax.broadcasted_iota(jnp.int32, x.shape, 0)
    cw = cw_ref[...]
    x_m1 = jnp.where(row == 0, prev[7:8], pltpu.roll(x, 1, 0))
    x_m2 = jnp.where(row == 0, prev[6:7], jnp.where(row == 1, prev[7:8], pltpu.roll(x, 2, 0)))
    x_p1 = jnp.where(row == tt - 1, nxt[0:1], pltpu.roll(x, tt - 1, 0))
    xc = x_m2 * cw[0:1] + x_m1 * cw[1:2] + x * cw[2:3] + x_p1 * cw[3:4] + cb_ref[...]

    gates = jnp.dot(xc.astype(BF16), gw_ref[0], preferred_element_type=F32) + gb_ref[0]
    r = jax.nn.sigmoid(gates[:, :LRU_WIDTH])
    i = jax.nn.sigmoid(gates[:, LRU_WIDTH:])
    lam = lam_ref[0]
    softplus_neg = jnp.maximum(-lam, 0.0) + jnp.log(1.0 + jnp.exp(-jnp.abs(lam)))
    log_a = -LRU_C * r * softplus_neg
    a = jnp.exp(log_a)
    b = jnp.sqrt(1.0 - a * a) * (i * xc)

    sub = row & (SUBLANES - 1)

    def local_scan(a, b, reverse):
        shift = 1
        while shift < SUBLANES:
            if reverse:
                a_s = pltpu.roll(a, tt - shift, 0)
                b_s = pltpu.roll(b, tt - shift, 0)
                valid = sub < SUBLANES - shift
            else:
                a_s = pltpu.roll(a, shift, 0)
                b_s = pltpu.roll(b, shift, 0)
                valid = sub >= shift
            b = a * jnp.where(valid, b_s, 0.0) + b
            a = a * jnp.where(valid, a_s, 1.0)
            shift *= 2
        return a, b

    n_groups = tt // SUBLANES

    def sweep(reverse):
        a_l, b_l = local_scan(a, b, reverse)
        a_sc[...] = a_l
        b_sc[...] = b_l

        def step(g, carry):
            gi = (n_groups - 1 - g) if reverse else g
            rs = pl.ds(pl.multiple_of(gi * SUBLANES, SUBLANES), SUBLANES)
            h = b_sc[rs, :] + a_sc[rs, :] * carry
            h_ref[0, 0, rs, :] = h
            return h[0:1] if reverse else h[SUBLANES - 1:SUBLANES]

        carry_sc[...] = lax.fori_loop(0, n_groups, step, carry_sc[...])

    @pl.when(d == 0)
    def _():
        sweep(False)

    @pl.when(d == 1)
    def _():
        sweep(True)


def _lru(xr, conv_w, conv_b, gate_w, gate_b, lru_lam):
    B, S, W = xr.shape
    tt = _tile(S, 1024)
    n_t = S // tt
    hb = tt // SUBLANES
    n_hb = S // SUBLANES

    def tblk(d, j):
        return jnp.where(d == 0, j, n_t - 1 - j)

    return pl.pallas_call(
        _lru_body,
        grid=(B, 2, n_t),
        in_specs=[
            pl.BlockSpec((1, tt, W), lambda b, d, j: (b, tblk(d, j), 0)),
            pl.BlockSpec((1, SUBLANES, W),
                         lambda b, d, j: (b, jnp.maximum(tblk(d, j) * hb - 1, 0), 0)),
            pl.BlockSpec((1, SUBLANES, W),
                         lambda b, d, j: (b, jnp.minimum((tblk(d, j) + 1) * hb, n_hb - 1), 0)),
            pl.BlockSpec((4, W), lambda b, d, j: (0, 0)),
            pl.BlockSpec((1, W), lambda b, d, j: (0, 0)),
            pl.BlockSpec((1, W, 2 * W), lambda b, d, j: (d, 0, 0)),
            pl.BlockSpec((1, 1, 2 * W), lambda b, d, j: (d, 0, 0)),
            pl.BlockSpec((1, 1, W), lambda b, d, j: (d, 0, 0)),
        ],
        out_specs=pl.BlockSpec((1, 1, tt, W), lambda b, d, j: (b, d, tblk(d, j), 0)),
        out_shape=jax.ShapeDtypeStruct((B, 2, S, W), F32),
        scratch_shapes=[
            pltpu.VMEM((tt, W), F32),
            pltpu.VMEM((tt, W), F32),
            pltpu.VMEM((1, W), F32),
        ],
        compiler_params=pltpu.CompilerParams(
            dimension_semantics=("arbitrary", "arbitrary", "arbitrary")),
        name="lru",
    )(xr, xr, xr, conv_w, conv_b.reshape(1, W), gate_w, gate_b, lru_lam)


def _block_diag(w):
    eye = jnp.eye(LRU_BLOCKS, dtype=w.dtype)
    return jnp.einsum('nde,nm->ndme', w, eye).reshape(LRU_WIDTH, LRU_WIDTH)


def _outproj_body(x_ref, attn_ref, hf_ref, hb_ref, gate_ref, wo_ref, g_ref, b_ref, o_ref):
    rec = (hf_ref[0, 0] + hb_ref[0, 0]) * _gelu(gate_ref[0])
    mix = jnp.dot(attn_ref[0], wo_ref[:ATTN_WIDTH], preferred_element_type=F32)
    mix += jnp.dot(rec.astype(BF16), wo_ref[ATTN_WIDTH:], preferred_element_type=F32)
    o_ref[0] = _layer_norm(DN_ALPHA * x_ref[0] + mix, g_ref[...], b_ref[...])


def _outproj(x, attn, h, gate, w_out, ln_g, ln_b):
    B, S, D = x.shape
    tm = _tile(S, 512)
    row = lambda b, i: (b, i, 0)
    const = lambda b, i: (0, 0)
    return pl.pallas_call(
        _outproj_body,
        grid=(B, S // tm),
        in_specs=[
            pl.BlockSpec((1, tm, D), row),
            pl.BlockSpec((1, tm, ATTN_WIDTH), row),
            pl.BlockSpec((1, 1, tm, LRU_WIDTH), lambda b, i: (b, 0, i, 0)),
            pl.BlockSpec((1, 1, tm, LRU_WIDTH), lambda b, i: (b, 1, i, 0)),
            pl.BlockSpec((1, tm, LRU_WIDTH), row),
            pl.BlockSpec((ATTN_WIDTH + LRU_WIDTH, D), const),
            pl.BlockSpec((1, D), const),
            pl.BlockSpec((1, D), const),
        ],
        out_specs=pl.BlockSpec((1, tm, D), row),
        out_shape=jax.ShapeDtypeStruct((B, S, D), F32),
        compiler_params=pltpu.CompilerParams(
            dimension_semantics=("arbitrary", "arbitrary")),
        name="outproj",
    )(x, attn, h, h, gate, w_out, ln_g.reshape(1, D), ln_b.reshape(1, D))


_CAND_PAIRS = [(i, j) for i in range(PEER_TOPK + 1) for j in range(PEER_TOPK + 1)
               if (i + 1) * (j + 1) <= PEER_TOPK + 1]
_CAND_ROWS = -(-len(_CAND_PAIRS) // SUBLANES) * SUBLANES


def _batcher_pairs(n):
    def merge(lo, hi, r):
        step = 2 * r
        if step < hi - lo:
            yield from merge(lo, hi, step)
            yield from merge(lo + r, hi, step)
            yield from ((i, i + r) for i in range(lo + r, hi - r, step))
        else:
            yield (lo, lo + r)

    def sort(lo, hi):
        if hi - lo >= 1:
            mid = lo + (hi - lo) // 2
            yield from sort(lo, mid)
            yield from sort(mid + 1, hi)
            yield from merge(lo, hi, 1)

    return list(sort(0, n - 1))


def _top_values(s, n):
    v = [s[i * SUBLANES:(i + 1) * SUBLANES, :] for i in range(s.shape[0] // SUBLANES)]
    for i, j in _batcher_pairs(len(v)):
        v[i], v[j] = jnp.maximum(v[i], v[j]), jnp.minimum(v[i], v[j])
    out = []
    for r in range(n):
        mx = jnp.max(v[0], axis=0, keepdims=True)
        out.append(mx)
        depth = min(len(v), n - 1 - r)
        if depth:
            hit = v[0] == mx
            for i in range(depth):
                below = v[i + 1] if i + 1 < len(v) else NEG_INF
                v[i] = jnp.where(hit, below, v[i])
    return out


def _dup_bf16(x):
    u = pltpu.bitcast(x.astype(BF16).astype(F32), jnp.uint32)
    return u | lax.shift_right_logical(u, jnp.uint32(16))


def _route_body(x_ref, wq_ref, keys_ref, cnt_ref, ga_ref, rank_ref, gb_ref, cand_sc):
    xb = x_ref[...].astype(BF16)
    qp = jnp.dot(xb, wq_ref[...], preferred_element_type=F32)
    cand_sc[...] = jnp.full(cand_sc.shape, NEG_INF, F32)
    n = PEER_TOPK + 1
    for h in range(PEER_HEADS):
        s = []
        for p in range(2):
            c0 = (2 * h + p) * PEER_HALF
            blk = qp[:, c0:c0 + PEER_HALF].astype(BF16)
            s.append(lax.dot_general(keys_ref[p], blk, _NT, preferred_element_type=F32))
        s1, s2 = s
        a = _top_values(s1, n)
        b = _top_values(s2, n)
        rank2 = jnp.full(s2.shape, float(n), F32)
        for j in reversed(range(n)):
            rank2 = jnp.where(s2 >= b[j], float(j), rank2)
        for r, (i, j) in enumerate(_CAND_PAIRS):
            cand_sc[r:r + 1, :] = a[i] + b[j]
        cand = cand_sc[...]
        cur = cand
        for r in range(PEER_TOPK):
            mx = jnp.max(cur, axis=0, keepdims=True)
            cur = jnp.where(cur == mx, NEG_INF, cur)
        thr = 0.5 * (mx + jnp.max(cur, axis=0, keepdims=True))
        z = jnp.sum(jnp.where(cand >= thr, jnp.exp(cand - (a[0] + b[0])), 0.0),
                    axis=0, keepdims=True)
        need = thr - s1
        cnt = jnp.zeros(s1.shape, F32)
        for j in range(n):
            cnt = jnp.where(b[j] >= need, j + 1.0, cnt)
        outs = (
            (cnt_ref, _dup_bf16(cnt)),
            (ga_ref, _dup_bf16(jnp.exp(s1 - a[0]) / z)),
            (rank_ref, pltpu.bitcast(rank2.astype(BF16), jnp.uint32)),
            (gb_ref, pltpu.bitcast(jnp.exp(s2 - b[0]).astype(BF16), jnp.uint32)),
        )
        for ref, val in outs:
            for c in range(val.shape[1] // LANES):
                ref[c, h] = val[:, c * LANES:(c + 1) * LANES]


def _route(x1, wq, keys):
    T, D = x1.shape
    tm = _tile(T, 256)
    shape = (T // LANES, PEER_HEADS, N_KEYS, LANES)
    packed = (T // LANES, PEER_HEADS, N_KEYS // 2, LANES)
    ospec = pl.BlockSpec((tm // LANES,) + shape[1:], lambda i: (i, 0, 0, 0))
    pspec = pl.BlockSpec((tm // LANES,) + packed[1:], lambda i: (i, 0, 0, 0))
    return pl.pallas_call(
        _route_body,
        grid=(T // tm,),
        in_specs=[
            pl.BlockSpec((tm, D), lambda i: (i, 0)),
            pl.BlockSpec((D, 2 * PEER_HEADS * PEER_HALF), lambda i: (0, 0)),
            pl.BlockSpec((2, N_KEYS, PEER_HALF), lambda i: (0, 0, 0)),
        ],
        out_specs=[ospec, ospec, pspec, pspec],
        out_shape=[jax.ShapeDtypeStruct(shape, jnp.uint32), jax.ShapeDtypeStruct(shape, jnp.uint32),
                   jax.ShapeDtypeStruct(packed, jnp.uint32), jax.ShapeDtypeStruct(packed, jnp.uint32)],
        scratch_shapes=[pltpu.VMEM((_CAND_ROWS, tm), F32)],
        compiler_params=pltpu.CompilerParams(dimension_semantics=("arbitrary",)),
        name="route",
    )(x1, wq, keys)


PEER_SUB = 1024
PEER_SUBS_PER_STEP = 2
PEER_TOK = 256
PEER_TOK_PARTS = 4
PEER_SLAB = 16


def _peer_body(x_ref, cnt_ref, ga_ref, rank_ref, gb_ref, eu_ref, evt_ref, g_ref, b_ref, o_ref,
               xb_sc, act0_sc, act1_sc, wg0_sc, wg1_sc, acc_sc):
    j = pl.program_id(1)
    n_part = xb_sc.shape[0]
    n_stage = PEER_SUBS_PER_STEP * n_part
    assert n_part % 2 == 0 and n_part & (n_part - 1) == 0
    part_bits = n_part.bit_length() - 1
    act_sc = (act0_sc, act1_sc)
    wg_sc = (wg0_sc, wg1_sc)

    @pl.when(j == 0)
    def _():
        for t in range(n_part):
            xb_sc[t] = x_ref[t * PEER_TOK:(t + 1) * PEER_TOK, :].astype(BF16)
        acc_sc[...] = jnp.zeros(acc_sc.shape, F32)

    def split(i):
        if isinstance(i, int):
            return i // n_part, i % n_part
        return lax.shift_right_logical(i, part_bits), i & (n_part - 1)

    def stage_a(i, p):
        s, t = split(i)
        eu = pltpu.bitcast(eu_ref[s], BF16)
        act_sc[p][...] = lax.dot_general(eu, xb_sc[t], _NT, preferred_element_type=F32)

    kt = 2 * N_KEYS

    def stage_c(i, p):
        s, t = split(i)
        evt = pltpu.bitcast(evt_ref[s], BF16)
        part = None
        for q in range(PEER_SUB // kt):
            d = jnp.dot(evt[:, q * kt:(q + 1) * kt], wg_sc[p][q * kt:(q + 1) * kt, :],
                        preferred_element_type=F32)
            part = d if part is None else part + d
        acc_sc[t] += part

    def rows16(ref, ch, h, g):
        row = ref[ch, h, pl.ds(g, 1), :]
        return pltpu.bitcast(jnp.broadcast_to(row, (SUBLANES, LANES)), BF16)

    def stage_b(i, p):
        s, t = split(i)
        for gg in range(PEER_SUB // N_KEYS):
            g = s * (PEER_SUB // N_KEYS) + gg
            for lc in range(PEER_TOK // LANES):
                ch = t * (PEER_TOK // LANES) + lc
                ls = slice(lc * LANES, (lc + 1) * LANES)
                cnt = [rows16(cnt_ref, ch, h, g) for h in range(PEER_HEADS)]
                ga = [rows16(ga_ref, ch, h, g) for h in range(PEER_HEADS)]
                for sl in range(N_KEYS // PEER_SLAB):
                    rs = slice(sl * SUBLANES, (sl + 1) * SUBLANES)
                    es = slice(gg * N_KEYS + sl * PEER_SLAB, gg * N_KEYS + (sl + 1) * PEER_SLAB)
                    w = None
                    for h in range(PEER_HEADS):
                        rank = pltpu.bitcast(rank_ref[ch, h, rs, :], BF16)
                        gb = pltpu.bitcast(gb_ref[ch, h, rs, :], BF16)
                        term = jnp.where(rank < cnt[h], ga[h], 0.0) * gb
                        w = term if w is None else w + term
                    wg_sc[p][es, ls] = w * _gelu(act_sc[p][es, ls].astype(BF16))

    stage_a(0, 0)
    stage_a(1, 1)
    stage_b(0, 0)
    stage_c(0, 0)

    def steady(k, carry):
        i = 2 * k
        stage_a(i + 2, 0)
        stage_b(i + 1, 1)
        stage_c(i + 1, 1)
        stage_a(i + 3, 1)
        stage_b(i + 2, 0)
        stage_c(i + 2, 0)
        return carry

    lax.fori_loop(0, n_stage // 2 - 1, steady, 0)
    stage_b(n_stage - 1, 1)
    stage_c(n_stage - 1, 1)

    @pl.when(j == pl.num_programs(1) - 1)
    def _():
        for t in range(n_part):
            ts = slice(t * PEER_TOK, (t + 1) * PEER_TOK)
            y = DN_ALPHA * x_ref[ts, :] + acc_sc[t].T
            o_ref[ts, :] = _layer_norm(y, g_ref[...], b_ref[...])


def _pack_body(w_ref, o_ref, *, transpose):
    w = w_ref[...]
    if transpose:
        w = w.T
    o_ref[0] = pltpu.bitcast(w.astype(BF16), jnp.uint32)


def _pack_experts(w, transpose):
    n_exp, D = w.shape
    rows, cols = (D, PEER_SUB) if transpose else (PEER_SUB, D)
    return pl.pallas_call(
        functools.partial(_pack_body, transpose=transpose),
        grid=(n_exp // PEER_SUB,),
        in_specs=[pl.BlockSpec((PEER_SUB, D), lambda i: (i, 0))],
        out_specs=pl.BlockSpec((1, rows // 2, cols), lambda i: (i, 0, 0)),
        out_shape=jax.ShapeDtypeStruct((n_exp // PEER_SUB, rows // 2, cols), jnp.uint32),
        compiler_params=pltpu.CompilerParams(dimension_semantics=("arbitrary",)),
        name="pack",
    )(w)


def _peer(x1, cnt, ga, rank2, gb, eu3, evt3, ln_g, ln_b):
    T, D = x1.shape
    n_exp = eu3.shape[0] * PEER_SUB
    tc = PEER_TOK_PARTS * PEER_TOK
    if T % tc:
        tc = 2 * PEER_TOK
    assert T % tc == 0
    ns = PEER_SUBS_PER_STEP
    ne = ns * PEER_SUB
    gs = ne // N_KEYS
    nch = tc // LANES
    per_group = pl.BlockSpec((nch, PEER_HEADS, gs, LANES), lambda i, j: (i, 0, j, 0))
    per_key = pl.BlockSpec((nch, PEER_HEADS, N_KEYS // 2, LANES), lambda i, j: (i, 0, 0, 0))
    const = lambda i, j: (0, 0)
    return pl.pallas_call(
        _peer_body,
        grid=(T // tc, n_exp // ne),
        in_specs=[
            pl.BlockSpec((tc, D), lambda i, j: (i, 0)),
            per_group, per_group, per_key, per_key,
            pl.BlockSpec((ns, PEER_SUB // 2, D), lambda i, j: (j, 0, 0)),
            pl.BlockSpec((ns, D // 2, PEER_SUB), lambda i, j: (j, 0, 0)),
            pl.BlockSpec((1, D), const),
            pl.BlockSpec((1, D), const),
        ],
        out_specs=pl.BlockSpec((tc, D), lambda i, j: (i, 0)),
        out_shape=jax.ShapeDtypeStruct((T, D), F32),
        scratch_shapes=[
            pltpu.VMEM((tc // PEER_TOK, PEER_TOK, D), BF16),
            pltpu.VMEM((PEER_SUB, PEER_TOK), F32),
            pltpu.VMEM((PEER_SUB, PEER_TOK), F32),
            pltpu.VMEM((PEER_SUB, PEER_TOK), BF16),
            pltpu.VMEM((PEER_SUB, PEER_TOK), BF16),
            pltpu.VMEM((tc // PEER_TOK, D, PEER_TOK), F32),
        ],
        compiler_params=pltpu.CompilerParams(
            dimension_semantics=("arbitrary", "arbitrary")),
        name="peer",
    )(x1, cnt, ga, rank2, gb, eu3, evt3, ln_g.reshape(1, D), ln_b.reshape(1, D))


def _prepare_layer(l, w_in, lambda_qk, subln_g, conv_w, conv_b, gate_a_w, gate_a_b, gate_i_w,
                   gate_i_b, lru_lambda, w_out, ln1_g, ln1_b, peer_wq, peer_keys, expert_u,
                   expert_v, ln2_g, ln2_b):
    gate_w = jnp.stack([
        jnp.concatenate([_block_diag(gate_a_w[l, d]), _block_diag(gate_i_w[l, d])], axis=1)
        for d in range(2)]).astype(BF16)
    return dict(
        lambda_init=0.8 - 0.6 * math.exp(-0.3 * l),
        w_in=w_in[l].astype(BF16), lambda_qk=lambda_qk[l], subln_g=subln_g[l],
        conv_w=conv_w[l], conv_b=conv_b[l], gate_w=gate_w,
        gate_b=jnp.concatenate([gate_a_b[l], gate_i_b[l]], axis=-1)[:, None, :],
        lru_lambda=lru_lambda[l][:, None, :], w_out=w_out[l].astype(BF16),
        ln1_g=ln1_g[l], ln1_b=ln1_b[l], peer_wq=peer_wq[l].astype(BF16),
        peer_keys=peer_keys[l].astype(BF16), eu=_pack_experts(expert_u[l], False),
        evt=_pack_experts(expert_v[l], True), ln2_g=ln2_g[l], ln2_b=ln2_b[l])


def _layer(x, p, tables):
    B, S, D = x.shape
    q, k, v, xr, gate = _proj(x, p["w_in"], *tables)
    attn = _attention(q, k, v, p["lambda_qk"], p["subln_g"], p["lambda_init"])
    h = _lru(xr, p["conv_w"], p["conv_b"], p["gate_w"], p["gate_b"], p["lru_lambda"])
    x1 = _outproj(x, attn, h, gate, p["w_out"], p["ln1_g"], p["ln1_b"]).reshape(B * S, D)
    cnt, ga, rank2, gb = _route(x1, p["peer_wq"], p["peer_keys"])
    return _peer(x1, cnt, ga, rank2, gb, p["eu"], p["evt"], p["ln2_g"], p["ln2_b"]).reshape(B, S, D)


def _trunks(xs, *weights):
    layers = [_prepare_layer(l, *weights) for l in range(DEPTH)]
    outs = []
    for x in xs:
        tables = _rotary_tables(x.shape[1])
        for p in layers:
            x = _layer(x, p, tables)
        outs.append(x)
    return tuple(outs)


def _trunk(x, *weights):
    return _trunks((x,), *weights)[0]


def kernel(x_prompt, x_sample, w_in, lambda_qk, subln_g, conv_w, conv_b, gate_a_w, gate_a_b,
           gate_i_w, gate_i_b, lru_lambda, w_out, ln1_g, ln1_b, peer_wq, peer_keys, expert_u,
           expert_v, ln2_g, ln2_b):
    return _trunks((x_prompt, x_sample), w_in, lambda_qk, subln_g, conv_w, conv_b, gate_a_w,
                   gate_a_b, gate_i_w, gate_i_b, lru_lambda, w_out, ln1_g, ln1_b, peer_wq,
                   peer_keys, expert_u, expert_v, ln2_g, ln2_b)
```

```python
import functools
import math

import jax
import jax.numpy as jnp
from jax import lax
from jax.experimental import pallas as pl
from jax.experimental.pallas import tpu as pltpu

F32 = jnp.float32
BF16 = jnp.bfloat16

D_MODEL = 1024
DEPTH = 2
ATTN_WIDTH = 512
LRU_WIDTH = 512
N_DIFF_HEADS = 4
QK_HEAD_DIM = 64
V_HEAD_DIM = 128
ROT_DIM = 16
ROPE_THETA = 500000.0
CONV_LEFT = 2
LRU_BLOCKS = 8
LRU_BLOCK_DIM = 64
LRU_C = 8.0
PEER_HEADS = 8
N_KEYS = 128
PEER_TOPK = 16
PEER_HALF = 128
IN_WIDTH = 3 * ATTN_WIDTH + 2 * LRU_WIDTH
DN_ALPHA = (2 * DEPTH) ** 0.25
LN_EPS = 1e-5
SUBLN_EPS = 1e-5

LANES = 128
SUBLANES = 8
NEG_INF = float("-inf")
LOG2_E = math.log2(math.e)

_NT = (((1,), (1,)), ((), ()))


def _tile(n, want):
    t = min(n, want)
    assert n % t == 0, (n, t)
    return t


def _layer_norm(y, g, b):
    mu = jnp.mean(y, axis=-1, keepdims=True)
    yc = y - mu
    var = jnp.mean(yc * yc, axis=-1, keepdims=True)
    return yc * lax.rsqrt(var + LN_EPS) * g + b


def _gelu(x):
    c = math.sqrt(2.0 / math.pi)
    hx = 0.5 * x
    return hx + hx * jnp.tanh(x * (c + (c * 0.044715) * (x * x)))


def _proj_body(x_ref, w_ref, cos_ref, sa_ref, sb_ref, q_ref, k_ref, v_ref, xr_ref, gate_ref):
    xb = x_ref[0].astype(BF16)
    cos = cos_ref[...]
    sa = sa_ref[...]
    sb = sb_ref[...]

    def rotate(t):
        return t * cos + pltpu.roll(t, 8, 1) * sa + pltpu.roll(t, LANES - 8, 1) * sb

    def proj(c0, width):
        return jnp.dot(xb, w_ref[:, c0:c0 + width], preferred_element_type=F32)

    q = proj(0, ATTN_WIDTH)
    k = proj(ATTN_WIDTH, ATTN_WIDTH)
    for h in range(N_DIFF_HEADS):
        cs = slice(h * LANES, (h + 1) * LANES)
        q_ref[0, :, cs] = (rotate(q[:, cs]) * (QK_HEAD_DIM ** -0.5 * LOG2_E)).astype(BF16)
        k_ref[0, :, cs] = rotate(k[:, cs]).astype(BF16)
    v_ref[0] = proj(2 * ATTN_WIDTH, ATTN_WIDTH).astype(BF16)
    xr_ref[0] = proj(3 * ATTN_WIDTH, LRU_WIDTH).astype(BF16)
    gate_ref[0] = proj(3 * ATTN_WIDTH + LRU_WIDTH, LRU_WIDTH).astype(BF16)


def _proj(x, w_in, cos_t, sa_t, sb_t):
    B, S, D = x.shape
    tm = _tile(S, 512)
    row = lambda b, i: (b, i, 0)
    tab = lambda b, i: (i, 0)
    out_bf = jax.ShapeDtypeStruct((B, S, ATTN_WIDTH), BF16)
    return pl.pallas_call(
        _proj_body,
        grid=(B, S // tm),
        in_specs=[
            pl.BlockSpec((1, tm, D), row),
            pl.BlockSpec((D, IN_WIDTH), lambda b, i: (0, 0)),
            pl.BlockSpec((tm, LANES), tab),
            pl.BlockSpec((tm, LANES), tab),
            pl.BlockSpec((tm, LANES), tab),
        ],
        out_specs=[pl.BlockSpec((1, tm, ATTN_WIDTH), row)] * 5,
        out_shape=[out_bf] * 5,
        compiler_params=pltpu.CompilerParams(
            dimension_semantics=("arbitrary", "arbitrary")),
        name="proj",
    )(x, w_in, cos_t, sa_t, sb_t)


def _rotary_tables(S):
    half = ROT_DIM // 2
    inv = ROPE_THETA ** (-jnp.arange(0, ROT_DIM, 2, dtype=F32) / ROT_DIM)
    ang = jnp.arange(S, dtype=F32)[:, None] * inv[None, :]
    cos, sin = jnp.cos(ang), jnp.sin(ang)
    ones = jnp.ones((S, QK_HEAD_DIM - ROT_DIM), F32)
    zeros = jnp.zeros((S, QK_HEAD_DIM - ROT_DIM), F32)
    z8 = jnp.zeros((S, half), F32)
    cos_t = jnp.concatenate([cos, cos, ones], axis=-1)
    sa_t = jnp.concatenate([z8, sin, zeros], axis=-1)
    sb_t = jnp.concatenate([-sin, z8, zeros], axis=-1)
    rep = lambda t: jnp.concatenate([t, t], axis=-1)
    return rep(cos_t), rep(sa_t), rep(sb_t)


def _attn_body(lam_ref, g_ref, q_ref, k_ref, v_ref, o_ref, km_sc, va_sc, m_sc, acc_sc, *,
               lambda_init, tk):
    seq = k_ref.shape[1]

    @pl.when(pl.program_id(2) == 0)
    def _():
        k = k_ref[0]
        lane = lax.broadcasted_iota(jnp.int32, k.shape, 1)
        zero = jnp.zeros_like(k)
        km_sc[0] = jnp.where(lane < QK_HEAD_DIM, k, zero)
        km_sc[1] = jnp.where(lane >= QK_HEAD_DIM, k, zero)
        va_sc[:, :V_HEAD_DIM] = v_ref[0]
        va_sc[:, V_HEAD_DIM:] = jnp.ones((seq, V_HEAD_DIM), BF16)

    m_sc[...] = jnp.full(m_sc.shape, NEG_INF, F32)
    acc_sc[...] = jnp.zeros(acc_sc.shape, F32)
    q = q_ref[0]

    def chunk(j, carry):
        ks = pl.ds(pl.multiple_of(j * tk, tk), tk)
        for c in range(2):
            s = lax.dot_general(q, km_sc[c, ks, :], _NT, preferred_element_type=F32)
            m_prev = m_sc[c]
            m_new = jnp.maximum(m_prev, jnp.max(s, axis=1, keepdims=True))
            alpha = jnp.exp2(m_prev - m_new)
            p = jnp.exp2(s - jnp.tile(m_new, (1, tk // LANES)))
            pv = jnp.dot(p.astype(BF16), va_sc[ks, :], preferred_element_type=F32)
            acc_sc[c] = jnp.tile(alpha, (1, 2)) * acc_sc[c] + pv
            m_sc[c] = m_new
        return carry

    lax.fori_loop(0, seq // tk, chunk, 0, unroll=True)

    lp = lam_ref[...]
    d1 = jnp.sum(lp[0:1] * lp[1:2], axis=1, keepdims=True)
    d2 = jnp.sum(lp[2:3] * lp[3:4], axis=1, keepdims=True)
    lam = jnp.exp(d1) - jnp.exp(d2) + lambda_init
    acc1 = acc_sc[0]
    acc2 = acc_sc[1]
    o = (acc1[:, :V_HEAD_DIM] / acc1[:, V_HEAD_DIM:]
         - lam * (acc2[:, :V_HEAD_DIM] / acc2[:, V_HEAD_DIM:]))
    o = o * lax.rsqrt(jnp.mean(o * o, axis=1, keepdims=True) + SUBLN_EPS) * g_ref[...]
    o_ref[0] = (o * (1.0 - lambda_init)).astype(o_ref.dtype)


def _attention(q, k, v, lambda_qk, subln_g, lambda_init):
    B, S, _ = q.shape
    tq = _tile(S, 1024)
    tk = _tile(S, 512)
    return pl.pallas_call(
        functools.partial(_attn_body, lambda_init=lambda_init, tk=tk),
        grid=(B, N_DIFF_HEADS, S // tq),
        in_specs=[
            pl.BlockSpec((4, QK_HEAD_DIM), lambda b, h, i: (0, 0)),
            pl.BlockSpec((1, V_HEAD_DIM), lambda b, h, i: (0, 0)),
            pl.BlockSpec((1, tq, LANES), lambda b, h, i: (b, i, h)),
            pl.BlockSpec((1, S, LANES), lambda b, h, i: (b, 0, h)),
            pl.BlockSpec((1, S, LANES), lambda b, h, i: (b, 0, h)),
        ],
        out_specs=pl.BlockSpec((1, tq, LANES), lambda b, h, i: (b, i, h)),
        out_shape=jax.ShapeDtypeStruct((B, S, ATTN_WIDTH), BF16),
        scratch_shapes=[
            pltpu.VMEM((2, S, LANES), BF16),
            pltpu.VMEM((S, 2 * V_HEAD_DIM), BF16),
            pltpu.VMEM((2, tq, LANES), F32),
            pltpu.VMEM((2, tq, 2 * V_HEAD_DIM), F32),
        ],
        compiler_params=pltpu.CompilerParams(
            dimension_semantics=("arbitrary", "arbitrary", "arbitrary")),
        name="attn",
    )(lambda_qk, subln_g.reshape(1, V_HEAD_DIM), q, k, v)


def _lru_body(x_ref, prev_ref, next_ref, cw_ref, cb_ref, gw_ref, gb_ref, lam_ref, h_ref,
              a_sc, b_sc, carry_sc):
    d = pl.program_id(1)
    j = pl.program_id(2)
    n_t = pl.num_programs(2)
    tt = x_ref.shape[1]
    jj = jnp.where(d == 0, j, n_t - 1 - j)

    @pl.when(j == 0)
    def _():
        carry_sc[...] = jnp.zeros(carry_sc.shape, F32)

    x = x_ref[0].astype(F32)
    hr = prev_ref.shape[1]
    prev = prev_ref[0].astype(F32) * (jj > 0).astype(F32)
    nxt = next_ref[0].astype(F32) * (jj < n_t - 1).astype(F32)
    row = lax.broadcasted_iota(jnp.int32, x.shape, 0)
    cw = cw_ref[...]
    p1 = prev[hr - 1:hr]
    p2 = prev[hr - 2:hr - 1]
    x_m1 = jnp.where(row == 0, p1, pltpu.roll(x, 1, 0))
    x_m2 = jnp.where(row == 0, p2, jnp.where(row == 1, p1, pltpu.roll(x, 2, 0)))
    x_p1 = jnp.where(row == tt - 1, nxt[0:1], pltpu.roll(x, tt - 1, 0))
    xc = x_m2 * cw[0:1] + x_m1 * cw[1:2] + x * cw[2:3] + x_p1 * cw[3:4] + cb_ref[...]

    gates = jnp.dot(xc.astype(BF16), gw_ref[0], preferred_element_type=F32) + gb_ref[0]
    r = jax.nn.sigmoid(gates[:, :LRU_WIDTH])
    i = jax.nn.sigmoid(gates[:, LRU_WIDTH:])
    lam = lam_ref[0]
    softplus_neg = jnp.maximum(-lam, 0.0) + jnp.log(1.0 + jnp.exp(-jnp.abs(lam)))
    log_a = -LRU_C * r * softplus_neg
    a = jnp.exp(log_a)
    b = jnp.sqrt(1.0 - a * a) * (i * xc)

    sub = row & (SUBLANES - 1)

    def local_scan(a, b, reverse):
        shift = 1
        while shift < SUBLANES:
            if reverse:
                a_s = pltpu.roll(a, tt - shift, 0)
                b_s = pltpu.roll(b, tt - shift, 0)
                valid = sub < SUBLANES - shift
            else:
                a_s = pltpu.roll(a, shift, 0)
                b_s = pltpu.roll(b, shift, 0)
                valid = sub >= shift
            b = a * jnp.where(valid, b_s, 0.0) + b
            a = a * jnp.where(valid, a_s, 1.0)
            shift *= 2
        return a, b

    n_groups = tt // SUBLANES

    def sweep(reverse):
        a_l, b_l = local_scan(a, b, reverse)
        a_sc[...] = a_l
        b_sc[...] = b_l

        def step(g, carry):
            gi = (n_groups - 1 - g) if reverse else g
            rs = pl.ds(pl.multiple_of(gi * SUBLANES, SUBLANES), SUBLANES)
            h = b_sc[rs, :] + a_sc[rs, :] * carry
            h_ref[0, 0, rs, :] = h
            return h[0:1] if reverse else h[SUBLANES - 1:SUBLANES]

        carry_sc[...] = lax.fori_loop(0, n_groups, step, carry_sc[...])

    @pl.when(d == 0)
    def _():
        sweep(False)

    @pl.when(d == 1)
    def _():
        sweep(True)


def _lru(xr, conv_w, conv_b, gate_w, gate_b, lru_lam):
    B, S, W = xr.shape
    tt = _tile(S, 1024)
    n_t = S // tt
    halo = 2 * SUBLANES
    hb = tt // halo
    n_hb = S // halo

    def tblk(d, j):
        return jnp.where(d == 0, j, n_t - 1 - j)

    return pl.pallas_call(
        _lru_body,
        grid=(B, 2, n_t),
        in_specs=[
            pl.BlockSpec((1, tt, W), lambda b, d, j: (b, tblk(d, j), 0)),
            pl.BlockSpec((1, halo, W),
                         lambda b, d, j: (b, jnp.maximum(tblk(d, j) * hb - 1, 0), 0)),
            pl.BlockSpec((1, halo, W),
                         lambda b, d, j: (b, jnp.minimum((tblk(d, j) + 1) * hb, n_hb - 1), 0)),
            pl.BlockSpec((4, W), lambda b, d, j: (0, 0)),
            pl.BlockSpec((1, W), lambda b, d, j: (0, 0)),
            pl.BlockSpec((1, W, 2 * W), lambda b, d, j: (d, 0, 0)),
            pl.BlockSpec((1, 1, 2 * W), lambda b, d, j: (d, 0, 0)),
            pl.BlockSpec((1, 1, W), lambda b, d, j: (d, 0, 0)),
        ],
        out_specs=pl.BlockSpec((1, 1, tt, W), lambda b, d, j: (b, d, tblk(d, j), 0)),
        out_shape=jax.ShapeDtypeStruct((B, 2, S, W), F32),
        scratch_shapes=[
            pltpu.VMEM((tt, W), F32),
            pltpu.VMEM((tt, W), F32),
            pltpu.VMEM((1, W), F32),
        ],
        compiler_params=pltpu.CompilerParams(
            dimension_semantics=("arbitrary", "arbitrary", "arbitrary")),
        name="lru",
    )(xr, xr, xr, conv_w, conv_b.reshape(1, W), gate_w, gate_b, lru_lam)


def _block_diag(w):
    eye = jnp.eye(LRU_BLOCKS, dtype=w.dtype)
    return jnp.einsum('nde,nm->ndme', w, eye).reshape(LRU_WIDTH, LRU_WIDTH)


def _outproj_body(x_ref, attn_ref, hf_ref, hb_ref, gate_ref, wo_ref, g_ref, b_ref, o_ref):
    rec = (hf_ref[0, 0] + hb_ref[0, 0]) * _gelu(gate_ref[0].astype(F32))
    mix = jnp.dot(attn_ref[0], wo_ref[:ATTN_WIDTH], preferred_element_type=F32)
    mix += jnp.dot(rec.astype(BF16), wo_ref[ATTN_WIDTH:], preferred_element_type=F32)
    o_ref[0] = _layer_norm(DN_ALPHA * x_ref[0] + mix, g_ref[...], b_ref[...])


def _outproj(x, attn, h, gate, w_out, ln_g, ln_b):
    B, S, D = x.shape
    tm = _tile(S, 512)
    row = lambda b, i: (b, i, 0)
    const = lambda b, i: (0, 0)
    return pl.pallas_call(
        _outproj_body,
        grid=(B, S // tm),
        in_specs=[
            pl.BlockSpec((1, tm, D), row),
            pl.BlockSpec((1, tm, ATTN_WIDTH), row),
            pl.BlockSpec((1, 1, tm, LRU_WIDTH), lambda b, i: (b, 0, i, 0)),
            pl.BlockSpec((1, 1, tm, LRU_WIDTH), lambda b, i: (b, 1, i, 0)),
            pl.BlockSpec((1, tm, LRU_WIDTH), row),
            pl.BlockSpec((ATTN_WIDTH + LRU_WIDTH, D), const),
            pl.BlockSpec((1, D), const),
            pl.BlockSpec((1, D), const),
        ],
        out_specs=pl.BlockSpec((1, tm, D), row),
        out_shape=jax.ShapeDtypeStruct((B, S, D), F32),
        compiler_params=pltpu.CompilerParams(
            dimension_semantics=("arbitrary", "arbitrary")),
        name="outproj",
    )(x, attn, h, h, gate, w_out, ln_g.reshape(1, D), ln_b.reshape(1, D))


_CAND_PAIRS = [(i, j) for i in range(PEER_TOPK + 1) for j in range(PEER_TOPK + 1)
               if (i + 1) * (j + 1) <= PEER_TOPK + 1]
_CAND_ROWS = -(-len(_CAND_PAIRS) // SUBLANES) * SUBLANES


def _batcher_pairs(n):
    def merge(lo, hi, r):
        step = 2 * r
        if step < hi - lo:
            yield from merge(lo, hi, step)
            yield from merge(lo + r, hi, step)
            yield from ((i, i + r) for i in range(lo + r, hi - r, step))
        else:
            yield (lo, lo + r)

    def sort(lo, hi):
        if hi - lo >= 1:
            mid = lo + (hi - lo) // 2
            yield from sort(lo, mid)
            yield from sort(mid + 1, hi)
            yield from merge(lo, hi, 1)

    return list(sort(0, n - 1))


def _top_values(s, n):
    v = [s[i * SUBLANES:(i + 1) * SUBLANES, :] for i in range(s.shape[0] // SUBLANES)]
    for i, j in _batcher_pairs(len(v)):
        v[i], v[j] = jnp.maximum(v[i], v[j]), jnp.minimum(v[i], v[j])
    out = []
    for r in range(n):
        mx = jnp.max(v[0], axis=0, keepdims=True)
        out.append(mx)
        depth = min(len(v), n - 1 - r)
        if depth:
            hit = v[0] == mx
            for i in range(depth):
                below = v[i + 1] if i + 1 < len(v) else NEG_INF
                v[i] = jnp.where(hit, below, v[i])
    return out


def _dup_bf16(x):
    u = pltpu.bitcast(x.astype(BF16).astype(F32), jnp.uint32)
    return u | lax.shift_right_logical(u, jnp.uint32(16))


def _route_body(x_ref, wq_ref, keys_ref, cnt_ref, ga_ref, rank_ref, gb_ref, cand_sc):
    xb = x_ref[...].astype(BF16)
    qp = jnp.dot(xb, wq_ref[...], preferred_element_type=F32)
    cand_sc[...] = jnp.full(cand_sc.shape, NEG_INF, F32)
    n = PEER_TOPK + 1
    for h in range(PEER_HEADS):
        s = []
        for p in range(2):
            c0 = (2 * h + p) * PEER_HALF
            blk = qp[:, c0:c0 + PEER_HALF].astype(BF16)
            s.append(lax.dot_general(keys_ref[p], blk, _NT, preferred_element_type=F32))
        s1, s2 = s
        a = _top_values(s1, n)
        b = _top_values(s2, n)
        rank2 = jnp.full(s2.shape, float(n), F32)
        for j in reversed(range(n)):
            rank2 = jnp.where(s2 >= b[j], float(j), rank2)
        for r, (i, j) in enumerate(_CAND_PAIRS):
            cand_sc[r:r + 1, :] = a[i] + b[j]
        cand = cand_sc[...]
        cur = cand
        for r in range(PEER_TOPK):
            mx = jnp.max(cur, axis=0, keepdims=True)
            cur = jnp.where(cur == mx, NEG_INF, cur)
        thr = 0.5 * (mx + jnp.max(cur, axis=0, keepdims=True))
        z = jnp.sum(jnp.where(cand >= thr, jnp.exp(cand - (a[0] + b[0])), 0.0),
                    axis=0, keepdims=True)
        need = thr - s1
        cnt = jnp.zeros(s1.shape, F32)
        for j in range(n):
            cnt = jnp.where(b[j] >= need, j + 1.0, cnt)
        outs = (
            (cnt_ref, _dup_bf16(cnt)),
            (ga_ref, _dup_bf16(jnp.exp(s1 - a[0]) / z)),
            (rank_ref, pltpu.bitcast(rank2.astype(BF16), jnp.uint32)),
            (gb_ref, pltpu.bitcast(jnp.exp(s2 - b[0]).astype(BF16), jnp.uint32)),
        )
        for ref, val in outs:
            for c in range(val.shape[1] // LANES):
                ref[c, h] = val[:, c * LANES:(c + 1) * LANES]


def _route(x1, wq, keys):
    T, D = x1.shape
    tm = _tile(T, 256)
    shape = (T // LANES, PEER_HEADS, N_KEYS, LANES)
    packed = (T // LANES, PEER_HEADS, N_KEYS // 2, LANES)
    ospec = pl.BlockSpec((tm // LANES,) + shape[1:], lambda i: (i, 0, 0, 0))
    pspec = pl.BlockSpec((tm // LANES,) + packed[1:], lambda i: (i, 0, 0, 0))
    return pl.pallas_call(
        _route_body,
        grid=(T // tm,),
        in_specs=[
            pl.BlockSpec((tm, D), lambda i: (i, 0)),
            pl.BlockSpec((D, 2 * PEER_HEADS * PEER_HALF), lambda i: (0, 0)),
            pl.BlockSpec((2, N_KEYS, PEER_HALF), lambda i: (0, 0, 0)),
        ],
        out_specs=[ospec, ospec, pspec, pspec],
        out_shape=[jax.ShapeDtypeStruct(shape, jnp.uint32), jax.ShapeDtypeStruct(shape, jnp.uint32),
                   jax.ShapeDtypeStruct(packed, jnp.uint32), jax.ShapeDtypeStruct(packed, jnp.uint32)],
        scratch_shapes=[pltpu.VMEM((_CAND_ROWS, tm), F32)],
        compiler_params=pltpu.CompilerParams(dimension_semantics=("arbitrary",)),
        name="route",
    )(x1, wq, keys)


PEER_SUB = 1024
PEER_SUBS_PER_STEP = 2
PEER_TOK = 256
PEER_TOK_PARTS = 4
PEER_SLAB = 16


def _peer_body(x_ref, cnt_ref, ga_ref, rank_ref, gb_ref, eu_ref, evt_ref, g_ref, b_ref, o_ref,
               xb_sc, act0_sc, act1_sc, wg0_sc, wg1_sc, acc_sc):
    j = pl.program_id(1)
    n_part = xb_sc.shape[0]
    n_stage = PEER_SUBS_PER_STEP * n_part
    assert n_part % 2 == 0 and n_part & (n_part - 1) == 0
    part_bits = n_part.bit_length() - 1
    act_sc = (act0_sc, act1_sc)
    wg_sc = (wg0_sc, wg1_sc)

    @pl.when(j == 0)
    def _():
        for t in range(n_part):
            xb_sc[t] = x_ref[t * PEER_TOK:(t + 1) * PEER_TOK, :].astype(BF16)
        acc_sc[...] = jnp.zeros(acc_sc.shape, F32)

    def split(i):
        if isinstance(i, int):
            return i // n_part, i % n_part
        return lax.shift_right_logical(i, part_bits), i & (n_part - 1)

    def stage_a(i, p):
        s, t = split(i)
        eu = pltpu.bitcast(eu_ref[s], BF16)
        act_sc[p][...] = lax.dot_general(eu, xb_sc[t], _NT, preferred_element_type=F32)

    def stage_c(i, p):
        s, t = split(i)
        evt = pltpu.bitcast(evt_ref[s], BF16)
        acc_sc[t] += jnp.dot(evt, wg_sc[p][...], preferred_element_type=F32)

    def rows16(ref, ch, h, g):
        row = ref[ch, h, pl.ds(g, 1), :]
        return pltpu.bitcast(jnp.broadcast_to(row, (SUBLANES, LANES)), BF16)

    def stage_b(i, p):
        s, t = split(i)
        for gg in range(PEER_SUB // N_KEYS):
            g = s * (PEER_SUB // N_KEYS) + gg
            for lc in range(PEER_TOK // LANES):
                ch = t * (PEER_TOK // LANES) + lc
                ls = slice(lc * LANES, (lc + 1) * LANES)
                cnt = [rows16(cnt_ref, ch, h, g) for h in range(PEER_HEADS)]
                ga = [rows16(ga_ref, ch, h, g) for h in range(PEER_HEADS)]
                for sl in range(N_KEYS // PEER_SLAB):
                    rs = slice(sl * SUBLANES, (sl + 1) * SUBLANES)
                    es = slice(gg * N_KEYS + sl * PEER_SLAB, gg * N_KEYS + (sl + 1) * PEER_SLAB)
                    w = None
                    for h in range(PEER_HEADS):
                        rank = pltpu.bitcast(rank_ref[ch, h, rs, :], BF16)
                        gb = pltpu.bitcast(gb_ref[ch, h, rs, :], BF16)
                        term = jnp.where(rank < cnt[h], ga[h], 0.0) * gb
                        w = term if w is None else w + term
                    wg_sc[p][es, ls] = w * _gelu(act_sc[p][es, ls].astype(BF16))

    stage_a(0, 0)
    stage_a(1, 1)
    stage_b(0, 0)

    def steady(k, carry):
        i = 2 * k
        stage_a(i + 2, 0)
        stage_b(i + 1, 1)
        stage_c(i, 0)
        stage_a(i + 3, 1)
        stage_b(i + 2, 0)
        stage_c(i + 1, 1)
        return carry

    lax.fori_loop(0, n_stage // 2 - 1, steady, 0, unroll=True)
    stage_b(n_stage - 1, 1)
    stage_c(n_stage - 2, 0)
    stage_c(n_stage - 1, 1)

    @pl.when(j == pl.num_programs(1) - 1)
    def _():
        for t in range(n_part):
            ts = slice(t * PEER_TOK, (t + 1) * PEER_TOK)
            y = DN_ALPHA * x_ref[ts, :] + acc_sc[t].T
            o_ref[ts, :] = _layer_norm(y, g_ref[...], b_ref[...])


def _pack_body(w_ref, o_ref, *, transpose):
    w = w_ref[...]
    if transpose:
        w = w.T
    o_ref[0] = pltpu.bitcast(w.astype(BF16), jnp.uint32)


def _pack_experts(w, transpose):
    n_exp, D = w.shape
    rows, cols = (D, PEER_SUB) if transpose else (PEER_SUB, D)
    return pl.pallas_call(
        functools.partial(_pack_body, transpose=transpose),
        grid=(n_exp // PEER_SUB,),
        in_specs=[pl.BlockSpec((PEER_SUB, D), lambda i: (i, 0))],
        out_specs=pl.BlockSpec((1, rows // 2, cols), lambda i: (i, 0, 0)),
        out_shape=jax.ShapeDtypeStruct((n_exp // PEER_SUB, rows // 2, cols), jnp.uint32),
        compiler_params=pltpu.CompilerParams(dimension_semantics=("arbitrary",)),
        name="pack",
    )(w)


def _peer(x1, cnt, ga, rank2, gb, eu3, evt3, ln_g, ln_b):
    T, D = x1.shape
    n_exp = eu3.shape[0] * PEER_SUB
    tc = PEER_TOK_PARTS * PEER_TOK
    if T % tc:
        tc = 2 * PEER_TOK
    assert T % tc == 0
    ns = PEER_SUBS_PER_STEP
    ne = ns * PEER_SUB
    gs = ne // N_KEYS
    nch = tc // LANES
    per_group = pl.BlockSpec((nch, PEER_HEADS, gs, LANES), lambda i, j: (i, 0, j, 0))
    per_key = pl.BlockSpec((nch, PEER_HEADS, N_KEYS // 2, LANES), lambda i, j: (i, 0, 0, 0))
    const = lambda i, j: (0, 0)
    return pl.pallas_call(
        _peer_body,
        grid=(T // tc, n_exp // ne),
        in_specs=[
            pl.BlockSpec((tc, D), lambda i, j: (i, 0)),
            per_group, per_group, per_key, per_key,
            pl.BlockSpec((ns, PEER_SUB // 2, D), lambda i, j: (j, 0, 0)),
            pl.BlockSpec((ns, D // 2, PEER_SUB), lambda i, j: (j, 0, 0)),
            pl.BlockSpec((1, D), const),
            pl.BlockSpec((1, D), const),
        ],
        out_specs=pl.BlockSpec((tc, D), lambda i, j: (i, 0)),
        out_shape=jax.ShapeDtypeStruct((T, D), F32),
        scratch_shapes=[
            pltpu.VMEM((tc // PEER_TOK, PEER_TOK, D), BF16),
            pltpu.VMEM((PEER_SUB, PEER_TOK), F32),
            pltpu.VMEM((PEER_SUB, PEER_TOK), F32),
            pltpu.VMEM((PEER_SUB, PEER_TOK), BF16),
            pltpu.VMEM((PEER_SUB, PEER_TOK), BF16),
            pltpu.VMEM((tc // PEER_TOK, D, PEER_TOK), F32),
        ],
        compiler_params=pltpu.CompilerParams(
            dimension_semantics=("arbitrary", "arbitrary")),
        name="peer",
    )(x1, cnt, ga, rank2, gb, eu3, evt3, ln_g.reshape(1, D), ln_b.reshape(1, D))


def _prepare_layer(l, w_in, lambda_qk, subln_g, conv_w, conv_b, gate_a_w, gate_a_b, gate_i_w,
                   gate_i_b, lru_lambda, w_out, ln1_g, ln1_b, peer_wq, peer_keys, expert_u,
                   expert_v, ln2_g, ln2_b):
    gate_w = jnp.stack([
        jnp.concatenate([_block_diag(gate_a_w[l, d]), _block_diag(gate_i_w[l, d])], axis=1)
        for d in range(2)]).astype(BF16)
    return dict(
        lambda_init=0.8 - 0.6 * math.exp(-0.3 * l),
        w_in=w_in[l].astype(BF16), lambda_qk=lambda_qk[l], subln_g=subln_g[l],
        conv_w=conv_w[l], conv_b=conv_b[l], gate_w=gate_w,
        gate_b=jnp.concatenate([gate_a_b[l], gate_i_b[l]], axis=-1)[:, None, :],
        lru_lambda=lru_lambda[l][:, None, :], w_out=w_out[l].astype(BF16),
        ln1_g=ln1_g[l], ln1_b=ln1_b[l], peer_wq=peer_wq[l].astype(BF16),
        peer_keys=peer_keys[l].astype(BF16), eu=_pack_experts(expert_u[l], False),
        evt=_pack_experts(expert_v[l], True), ln2_g=ln2_g[l], ln2_b=ln2_b[l])


def _layer(x, p, tables):
    B, S, D = x.shape
    q, k, v, xr, gate = _proj(x, p["w_in"], *tables)
    attn = _attention(q, k, v, p["lambda_qk"], p["subln_g"], p["lambda_init"])
    h = _lru(xr, p["conv_w"], p["conv_b"], p["gate_w"], p["gate_b"], p["lru_lambda"])
    x1 = _outproj(x, attn, h, gate, p["w_out"], p["ln1_g"], p["ln1_b"]).reshape(B * S, D)
    cnt, ga, rank2, gb = _route(x1, p["peer_wq"], p["peer_keys"])
    return _peer(x1, cnt, ga, rank2, gb, p["eu"], p["evt"], p["ln2_g"], p["ln2_b"]).reshape(B, S, D)


def _trunks(xs, *weights):
    layers = [_prepare_layer(l, *weights) for l in range(DEPTH)]
    outs = []
    for x in xs:
        tables = _rotary_tables(x.shape[1])
        for p in layers:
            x = _layer(x, p, tables)
        outs.append(x)
    return tuple(outs)


def _trunk(x, *weights):
    return _trunks((x,), *weights)[0]


def kernel(x_prompt, x_sample, w_in, lambda_qk, subln_g, conv_w, conv_b, gate_a_w, gate_a_b,
           gate_i_w, gate_i_b, lru_lambda, w_out, ln1_g, ln1_b, peer_wq, peer_keys, expert_u,
           expert_v, ln2_g, ln2_b):
    return _trunks((x_prompt, x_sample), w_in, lambda_qk, subln_g, conv_w, conv_b, gate_a_w,
                   gate_a_b, gate_i_w, gate_i_b, lru_lambda, w_out, ln1_g, ln1_b, peer_wq,
                   peer_keys, expert_u, expert_v, ln2_g, ln2_b)
```

```python
import functools
import math

import jax
import jax.numpy as jnp
from jax import lax
from jax.experimental import pallas as pl
from jax.experimental.pallas import tpu as pltpu

F32 = jnp.float32
BF16 = jnp.bfloat16

D_MODEL = 1024
DEPTH = 2
ATTN_WIDTH = 512
LRU_WIDTH = 512
N_DIFF_HEADS = 4
QK_HEAD_DIM = 64
V_HEAD_DIM = 128
ROT_DIM = 16
ROPE_THETA = 500000.0
CONV_LEFT = 2
LRU_BLOCKS = 8
LRU_BLOCK_DIM = 64
LRU_C = 8.0
PEER_HEADS = 8
N_KEYS = 128
PEER_TOPK = 16
PEER_HALF = 128
IN_WIDTH = 3 * ATTN_WIDTH + 2 * LRU_WIDTH
DN_ALPHA = (2 * DEPTH) ** 0.25
LN_EPS = 1e-5
SUBLN_EPS = 1e-5

LANES = 128
SUBLANES = 8
NEG_INF = float("-inf")
LOG2_E = math.log2(math.e)

_NT = (((1,), (1,)), ((), ()))


def _tile(n, want):
    t = min(n, want)
    assert n % t == 0, (n, t)
    return t


def _layer_norm(y, g, b):
    mu = jnp.mean(y, axis=-1, keepdims=True)
    yc = y - mu
    var = jnp.mean(yc * yc, axis=-1, keepdims=True)
    return yc * lax.rsqrt(var + LN_EPS) * g + b


def _gelu(x):
    c = math.sqrt(2.0 / math.pi)
    hx = 0.5 * x
    return hx + hx * jnp.tanh(x * (c + (c * 0.044715) * (x * x)))


def _proj_body(x_ref, w_ref, cos_ref, sa_ref, sb_ref, q_ref, k_ref, v_ref, xr_ref, gate_ref):
    xb = x_ref[0].astype(BF16)
    cos = cos_ref[...]
    sa = sa_ref[...]
    sb = sb_ref[...]

    def rotate(t):
        return t * cos + pltpu.roll(t, 8, 1) * sa + pltpu.roll(t, LANES - 8, 1) * sb

    def proj(c0, width):
        return jnp.dot(xb, w_ref[:, c0:c0 + width], preferred_element_type=F32)

    q = proj(0, ATTN_WIDTH)
    k = proj(ATTN_WIDTH, ATTN_WIDTH)
    for h in range(N_DIFF_HEADS):
        cs = slice(h * LANES, (h + 1) * LANES)
        q_ref[0, :, cs] = (rotate(q[:, cs]) * (QK_HEAD_DIM ** -0.5 * LOG2_E)).astype(BF16)
        k_ref[0, :, cs] = rotate(k[:, cs]).astype(BF16)
    v_ref[0] = proj(2 * ATTN_WIDTH, ATTN_WIDTH).astype(BF16)
    xr_ref[0] = proj(3 * ATTN_WIDTH, LRU_WIDTH)
    gate_ref[0] = proj(3 * ATTN_WIDTH + LRU_WIDTH, LRU_WIDTH)


def _proj(x, w_in, cos_t, sa_t, sb_t):
    B, S, D = x.shape
    tm = _tile(S, 512)
    row = lambda b, i: (b, i, 0)
    tab = lambda b, i: (i, 0)
    out_bf = jax.ShapeDtypeStruct((B, S, ATTN_WIDTH), BF16)
    out_f = jax.ShapeDtypeStruct((B, S, LRU_WIDTH), F32)
    return pl.pallas_call(
        _proj_body,
        grid=(B, S // tm),
        in_specs=[
            pl.BlockSpec((1, tm, D), row),
            pl.BlockSpec((D, IN_WIDTH), lambda b, i: (0, 0)),
            pl.BlockSpec((tm, LANES), tab),
            pl.BlockSpec((tm, LANES), tab),
            pl.BlockSpec((tm, LANES), tab),
        ],
        out_specs=[pl.BlockSpec((1, tm, ATTN_WIDTH), row)] * 5,
        out_shape=[out_bf, out_bf, out_bf, out_f, out_f],
        compiler_params=pltpu.CompilerParams(
            dimension_semantics=("arbitrary", "arbitrary")),
        name="proj",
    )(x, w_in, cos_t, sa_t, sb_t)


def _rotary_tables(S):
    half = ROT_DIM // 2
    inv = ROPE_THETA ** (-jnp.arange(0, ROT_DIM, 2, dtype=F32) / ROT_DIM)
    ang = jnp.arange(S, dtype=F32)[:, None] * inv[None, :]
    cos, sin = jnp.cos(ang), jnp.sin(ang)
    ones = jnp.ones((S, QK_HEAD_DIM - ROT_DIM), F32)
    zeros = jnp.zeros((S, QK_HEAD_DIM - ROT_DIM), F32)
    z8 = jnp.zeros((S, half), F32)
    cos_t = jnp.concatenate([cos, cos, ones], axis=-1)
    sa_t = jnp.concatenate([z8, sin, zeros], axis=-1)
    sb_t = jnp.concatenate([-sin, z8, zeros], axis=-1)
    rep = lambda t: jnp.concatenate([t, t], axis=-1)
    return rep(cos_t), rep(sa_t), rep(sb_t)


def _attn_body(lam_ref, g_ref, q_ref, k_ref, v_ref, o_ref, km_sc, va_sc, m_sc, acc_sc, *,
               lambda_init, tk):
    seq = k_ref.shape[1]

    @pl.when(pl.program_id(2) == 0)
    def _():
        k = k_ref[0]
        lane = lax.broadcasted_iota(jnp.int32, k.shape, 1)
        zero = jnp.zeros_like(k)
        km_sc[0] = jnp.where(lane < QK_HEAD_DIM, k, zero)
        km_sc[1] = jnp.where(lane >= QK_HEAD_DIM, k, zero)
        va_sc[:, :V_HEAD_DIM] = v_ref[0]
        va_sc[:, V_HEAD_DIM:] = jnp.ones((seq, V_HEAD_DIM), BF16)

    m_sc[...] = jnp.full(m_sc.shape, NEG_INF, F32)
    acc_sc[...] = jnp.zeros(acc_sc.shape, F32)
    q = q_ref[0]

    def chunk(j, carry):
        ks = pl.ds(pl.multiple_of(j * tk, tk), tk)
        for c in range(2):
            s = lax.dot_general(q, km_sc[c, ks, :], _NT, preferred_element_type=F32)
            m_prev = m_sc[c]
            m_new = jnp.maximum(m_prev, jnp.max(s, axis=1, keepdims=True))
            alpha = jnp.exp2(m_prev - m_new)
            p = jnp.exp2(s - jnp.tile(m_new, (1, tk // LANES)))
            pv = jnp.dot(p.astype(BF16), va_sc[ks, :], preferred_element_type=F32)
            acc_sc[c] = jnp.tile(alpha, (1, 2)) * acc_sc[c] + pv
            m_sc[c] = m_new
        return carry

    lax.fori_loop(0, seq // tk, chunk, 0, unroll=True)

    lp = lam_ref[...]
    d1 = jnp.sum(lp[0:1] * lp[1:2], axis=1, keepdims=True)
    d2 = jnp.sum(lp[2:3] * lp[3:4], axis=1, keepdims=True)
    lam = jnp.exp(d1) - jnp.exp(d2) + lambda_init
    acc1 = acc_sc[0]
    acc2 = acc_sc[1]
    o = (acc1[:, :V_HEAD_DIM] / acc1[:, V_HEAD_DIM:]
         - lam * (acc2[:, :V_HEAD_DIM] / acc2[:, V_HEAD_DIM:]))
    o = o * lax.rsqrt(jnp.mean(o * o, axis=1, keepdims=True) + SUBLN_EPS) * g_ref[...]
    o_ref[0] = (o * (1.0 - lambda_init)).astype(o_ref.dtype)


def _attention(q, k, v, lambda_qk, subln_g, lambda_init):
    B, S, _ = q.shape
    tq = _tile(S, 1024)
    tk = _tile(S, 512)
    return pl.pallas_call(
        functools.partial(_attn_body, lambda_init=lambda_init, tk=tk),
        grid=(B, N_DIFF_HEADS, S // tq),
        in_specs=[
            pl.BlockSpec((4, QK_HEAD_DIM), lambda b, h, i: (0, 0)),
            pl.BlockSpec((1, V_HEAD_DIM), lambda b, h, i: (0, 0)),
            pl.BlockSpec((1, tq, LANES), lambda b, h, i: (b, i, h)),
            pl.BlockSpec((1, S, LANES), lambda b, h, i: (b, 0, h)),
            pl.BlockSpec((1, S, LANES), lambda b, h, i: (b, 0, h)),
        ],
        out_specs=pl.BlockSpec((1, tq, LANES), lambda b, h, i: (b, i, h)),
        out_shape=jax.ShapeDtypeStruct((B, S, ATTN_WIDTH), BF16),
        scratch_shapes=[
            pltpu.VMEM((2, S, LANES), BF16),
            pltpu.VMEM((S, 2 * V_HEAD_DIM), BF16),
            pltpu.VMEM((2, tq, LANES), F32),
            pltpu.VMEM((2, tq, 2 * V_HEAD_DIM), F32),
        ],
        compiler_params=pltpu.CompilerParams(
            dimension_semantics=("arbitrary", "arbitrary", "arbitrary")),
        name="attn",
    )(lambda_qk, subln_g.reshape(1, V_HEAD_DIM), q, k, v)


def _lru_body(x_ref, prev_ref, next_ref, cw_ref, cb_ref, gw_ref, gb_ref, lam_ref, h_ref,
              a_sc, b_sc, carry_sc):
    d = pl.program_id(1)
    j = pl.program_id(2)
    n_t = pl.num_programs(2)
    tt = x_ref.shape[1]
    jj = jnp.where(d == 0, j, n_t - 1 - j)

    @pl.when(j == 0)
    def _():
        carry_sc[...] = jnp.zeros(carry_sc.shape, F32)

    x = x_ref[0]
    prev = prev_ref[0] * (jj > 0).astype(F32)
    nxt = next_ref[0] * (jj < n_t - 1).astype(F32)
    row = lax.broadcasted_iota(jnp.int32, x.shape, 0)
    cw = cw_ref[...]
    x_m1 = jnp.where(row == 0, prev[7:8], pltpu.roll(x, 1, 0))
    x_m2 = jnp.where(row == 0, prev[6:7], jnp.where(row == 1, prev[7:8], pltpu.roll(x, 2, 0)))
    x_p1 = jnp.where(row == tt - 1, nxt[0:1], pltpu.roll(x, tt - 1, 0))
    xc = x_m2 * cw[0:1] + x_m1 * cw[1:2] + x * cw[2:3] + x_p1 * cw[3:4] + cb_ref[...]

    gates = jnp.dot(xc.astype(BF16), gw_ref[0], preferred_element_type=F32) + gb_ref[0]
    r = jax.nn.sigmoid(gates[:, :LRU_WIDTH])
    i = jax.nn.sigmoid(gates[:, LRU_WIDTH:])
    lam = lam_ref[0]
    softplus_neg = jnp.maximum(-lam, 0.0) + jnp.log(1.0 + jnp.exp(-jnp.abs(lam)))
    log_a = -LRU_C * r * softplus_neg
    a = jnp.exp(log_a)
    b = jnp.sqrt(1.0 - a * a) * (i * xc)

    sub = row & (SUBLANES - 1)

    def local_scan(a, b, reverse):
        shift = 1
        while shift < SUBLANES:
            if reverse:
                a_s = pltpu.roll(a, tt - shift, 0)
                b_s = pltpu.roll(b, tt - shift, 0)
                valid = sub < SUBLANES - shift
            else:
                a_s = pltpu.roll(a, shift, 0)
                b_s = pltpu.roll(b, shift, 0)
                valid = sub >= shift
            b = a * jnp.where(valid, b_s, 0.0) + b
            a = a * jnp.where(valid, a_s, 1.0)
            shift *= 2
        return a, b

    n_groups = tt // SUBLANES

    def sweep(reverse):
        a_l, b_l = local_scan(a, b, reverse)
        a_sc[...] = a_l
        b_sc[...] = b_l

        def step(g, carry):
            gi = (n_groups - 1 - g) if reverse else g
            rs = pl.ds(pl.multiple_of(gi * SUBLANES, SUBLANES), SUBLANES)
            h = b_sc[rs, :] + a_sc[rs, :] * carry
            h_ref[0, 0, rs, :] = h
            return h[0:1] if reverse else h[SUBLANES - 1:SUBLANES]

        carry_sc[...] = lax.fori_loop(0, n_groups, step, carry_sc[...])

    @pl.when(d == 0)
    def _():
        sweep(False)

    @pl.when(d == 1)
    def _():
        sweep(True)


def _lru(xr, conv_w, conv_b, gate_w, gate_b, lru_lam):
    B, S, W = xr.shape
    tt = _tile(S, 1024)
    n_t = S // tt
    hb = tt // SUBLANES
    n_hb = S // SUBLANES

    def tblk(d, j):
        return jnp.where(d == 0, j, n_t - 1 - j)

    return pl.pallas_call(
        _lru_body,
        grid=(B, 2, n_t),
        in_specs=[
            pl.BlockSpec((1, tt, W), lambda b, d, j: (b, tblk(d, j), 0)),
            pl.BlockSpec((1, SUBLANES, W),
                         lambda b, d, j: (b, jnp.maximum(tblk(d, j) * hb - 1, 0), 0)),
            pl.BlockSpec((1, SUBLANES, W),
                         lambda b, d, j: (b, jnp.minimum((tblk(d, j) + 1) * hb, n_hb - 1), 0)),
            pl.BlockSpec((4, W), lambda b, d, j: (0, 0)),
            pl.BlockSpec((1, W), lambda b, d, j: (0, 0)),
            pl.BlockSpec((1, W, 2 * W), lambda b, d, j: (d, 0, 0)),
            pl.BlockSpec((1, 1, 2 * W), lambda b, d, j: (d, 0, 0)),
            pl.BlockSpec((1, 1, W), lambda b, d, j: (d, 0, 0)),
        ],
        out_specs=pl.BlockSpec((1, 1, tt, W), lambda b, d, j: (b, d, tblk(d, j), 0)),
        out_shape=jax.ShapeDtypeStruct((B, 2, S, W), F32),
        scratch_shapes=[
            pltpu.VMEM((tt, W), F32),
            pltpu.VMEM((tt, W), F32),
            pltpu.VMEM((1, W), F32),
        ],
        compiler_params=pltpu.CompilerParams(
            dimension_semantics=("arbitrary", "arbitrary", "arbitrary")),
        name="lru",
    )(xr, xr, xr, conv_w, conv_b.reshape(1, W), gate_w, gate_b, lru_lam)


def _block_diag(w):
    eye = jnp.eye(LRU_BLOCKS, dtype=w.dtype)
    return jnp.einsum('nde,nm->ndme', w, eye).reshape(LRU_WIDTH, LRU_WIDTH)


def _outproj_body(x_ref, attn_ref, hf_ref, hb_ref, gate_ref, wo_ref, g_ref, b_ref, o_ref):
    rec = (hf_ref[0, 0] + hb_ref[0, 0]) * _gelu(gate_ref[0])
    mix = jnp.dot(attn_ref[0], wo_ref[:ATTN_WIDTH], preferred_element_type=F32)
    mix += jnp.dot(rec.astype(BF16), wo_ref[ATTN_WIDTH:], preferred_element_type=F32)
    o_ref[0] = _layer_norm(DN_ALPHA * x_ref[0] + mix, g_ref[...], b_ref[...])


def _outproj(x, attn, h, gate, w_out, ln_g, ln_b):
    B, S, D = x.shape
    tm = _tile(S, 512)
    row = lambda b, i: (b, i, 0)
    const = lambda b, i: (0, 0)
    return pl.pallas_call(
        _outproj_body,
        grid=(B, S // tm),
        in_specs=[
            pl.BlockSpec((1, tm, D), row),
            pl.BlockSpec((1, tm, ATTN_WIDTH), row),
            pl.BlockSpec((1, 1, tm, LRU_WIDTH), lambda b, i: (b, 0, i, 0)),
            pl.BlockSpec((1, 1, tm, LRU_WIDTH), lambda b, i: (b, 1, i, 0)),
            pl.BlockSpec((1, tm, LRU_WIDTH), row),
            pl.BlockSpec((ATTN_WIDTH + LRU_WIDTH, D), const),
            pl.BlockSpec((1, D), const),
            pl.BlockSpec((1, D), const),
        ],
        out_specs=pl.BlockSpec((1, tm, D), row),
        out_shape=jax.ShapeDtypeStruct((B, S, D), F32),
        compiler_params=pltpu.CompilerParams(
            dimension_semantics=("arbitrary", "arbitrary")),
        name="outproj",
    )(x, attn, h, h, gate, w_out, ln_g.reshape(1, D), ln_b.reshape(1, D))


_CAND_PAIRS = [(i, j) for i in range(PEER_TOPK + 1) for j in range(PEER_TOPK + 1)
               if (i + 1) * (j + 1) <= PEER_TOPK + 1]
_CAND_ROWS = -(-len(_CAND_PAIRS) // SUBLANES) * SUBLANES


def _batcher_pairs(n):
    def merge(lo, hi, r):
        step = 2 * r
        if step < hi - lo:
            yield from merge(lo, hi, step)
            yield from merge(lo + r, hi, step)
            yield from ((i, i + r) for i in range(lo + r, hi - r, step))
        else:
            yield (lo, lo + r)

    def sort(lo, hi):
        if hi - lo >= 1:
            mid = lo + (hi - lo) // 2
            yield from sort(lo, mid)
            yield from sort(mid + 1, hi)
            yield from merge(lo, hi, 1)

    return list(sort(0, n - 1))


def _top_values(s, n):
    v = [s[i * SUBLANES:(i + 1) * SUBLANES, :] for i in range(s.shape[0] // SUBLANES)]
    for i, j in _batcher_pairs(len(v)):
        v[i], v[j] = jnp.maximum(v[i], v[j]), jnp.minimum(v[i], v[j])
    out = []
    for r in range(n):
        mx = jnp.max(v[0], axis=0, keepdims=True)
        out.append(mx)
        depth = min(len(v), n - 1 - r)
        if depth:
            hit = v[0] == mx
            for i in range(depth):
                below = v[i + 1] if i + 1 < len(v) else NEG_INF
                v[i] = jnp.where(hit, below, v[i])
    return out


def _dup_bf16(x):
    u = pltpu.bitcast(x.astype(BF16).astype(F32), jnp.uint32)
    return u | lax.shift_right_logical(u, jnp.uint32(16))


def _route_body(x_ref, wq_ref, keys_ref, cnt_ref, ga_ref, rank_ref, gb_ref, cand_sc):
    xb = x_ref[...].astype(BF16)
    qp = jnp.dot(xb, wq_ref[...], preferred_element_type=F32)
    cand_sc[...] = jnp.full(cand_sc.shape, NEG_INF, F32)
    n = PEER_TOPK + 1
    for h in range(PEER_HEADS):
        s = []
        for p in range(2):
            c0 = (2 * h + p) * PEER_HALF
            blk = qp[:, c0:c0 + PEER_HALF].astype(BF16)
            s.append(lax.dot_general(keys_ref[p], blk, _NT, preferred_element_type=F32))
        s1, s2 = s
        a = _top_values(s1, n)
        b = _top_values(s2, n)
        rank2 = jnp.full(s2.shape, float(n), F32)
        for j in reversed(range(n)):
            rank2 = jnp.where(s2 >= b[j], float(j), rank2)
        for r, (i, j) in enumerate(_CAND_PAIRS):
            cand_sc[r:r + 1, :] = a[i] + b[j]
        cand = cand_sc[...]
        cur = cand
        for r in range(PEER_TOPK):
            mx = jnp.max(cur, axis=0, keepdims=True)
            cur = jnp.where(cur == mx, NEG_INF, cur)
        thr = 0.5 * (mx + jnp.max(cur, axis=0, keepdims=True))
        z = jnp.sum(jnp.where(cand >= thr, jnp.exp(cand - (a[0] + b[0])), 0.0),
                    axis=0, keepdims=True)
        need = thr - s1
        cnt = jnp.zeros(s1.shape, F32)
        for j in range(n):
            cnt = jnp.where(b[j] >= need, j + 1.0, cnt)
        outs = (
            (cnt_ref, _dup_bf16(cnt)),
            (ga_ref, _dup_bf16(jnp.exp(s1 - a[0]) / z)),
            (rank_ref, pltpu.bitcast(rank2.astype(BF16), jnp.uint32)),
            (gb_ref, pltpu.bitcast(jnp.exp(s2 - b[0]).astype(BF16), jnp.uint32)),
        )
        for ref, val in outs:
            for c in range(val.shape[1] // LANES):
                ref[c, h] = val[:, c * LANES:(c + 1) * LANES]


def _route(x1, wq, keys):
    T, D = x1.shape
    tm = _tile(T, 256)
    shape = (T // LANES, PEER_HEADS, N_KEYS, LANES)
    packed = (T // LANES, PEER_HEADS, N_KEYS // 2, LANES)
    ospec = pl.BlockSpec((tm // LANES,) + shape[1:], lambda i: (i, 0, 0, 0))
    pspec = pl.BlockSpec((tm // LANES,) + packed[1:], lambda i: (i, 0, 0, 0))
    return pl.pallas_call(
        _route_body,
        grid=(T // tm,),
        in_specs=[
            pl.BlockSpec((tm, D), lambda i: (i, 0)),
            pl.BlockSpec((D, 2 * PEER_HEADS * PEER_HALF), lambda i: (0, 0)),
            pl.BlockSpec((2, N_KEYS, PEER_HALF), lambda i: (0, 0, 0)),
        ],
        out_specs=[ospec, ospec, pspec, pspec],
        out_shape=[jax.ShapeDtypeStruct(shape, jnp.uint32), jax.ShapeDtypeStruct(shape, jnp.uint32),
                   jax.ShapeDtypeStruct(packed, jnp.uint32), jax.ShapeDtypeStruct(packed, jnp.uint32)],
        scratch_shapes=[pltpu.VMEM((_CAND_ROWS, tm), F32)],
        compiler_params=pltpu.CompilerParams(dimension_semantics=("arbitrary",)),
        name="route",
    )(x1, wq, keys)


PEER_SUB = 1024
PEER_SUBS_PER_STEP = 2
PEER_TOK = 256
PEER_TOK_PARTS = 4
PEER_SLAB = 16


def _peer_body(x_ref, cnt_ref, ga_ref, rank_ref, gb_ref, eu_ref, evt_ref, g_ref, b_ref, o_ref,
               xb_sc, act0_sc, act1_sc, wg0_sc, wg1_sc, acc_sc):
    j = pl.program_id(1)
    n_part = xb_sc.shape[0]
    n_stage = PEER_SUBS_PER_STEP * n_part
    assert n_part % 2 == 0 and n_part & (n_part - 1) == 0
    part_bits = n_part.bit_length() - 1
    act_sc = (act0_sc, act1_sc)
    wg_sc = (wg0_sc, wg1_sc)

    @pl.when(j == 0)
    def _():
        for t in range(n_part):
            xb_sc[t] = x_ref[t * PEER_TOK:(t + 1) * PEER_TOK, :].astype(BF16)
        acc_sc[...] = jnp.zeros(acc_sc.shape, F32)

    def split(i):
        if isinstance(i, int):
            return i // n_part, i % n_part
        return lax.shift_right_logical(i, part_bits), i & (n_part - 1)

    def stage_a(i, p):
        s, t = split(i)
        eu = pltpu.bitcast(eu_ref[s], BF16)
        act_sc[p][...] = lax.dot_general(eu, xb_sc[t], _NT, preferred_element_type=F32)

    def stage_c(i, p):
        s, t = split(i)
        evt = pltpu.bitcast(evt_ref[s], BF16)
        acc_sc[t] += jnp.dot(evt, wg_sc[p][...], preferred_element_type=F32)

    def rows16(ref, ch, h, g):
        row = ref[ch, h, pl.ds(g, 1), :]
        return pltpu.bitcast(jnp.broadcast_to(row, (SUBLANES, LANES)), BF16)

    def stage_b(i, p):
        s, t = split(i)
        for gg in range(PEER_SUB // N_KEYS):
            g = s * (PEER_SUB // N_KEYS) + gg
            for lc in range(PEER_TOK // LANES):
                ch = t * (PEER_TOK // LANES) + lc
                ls = slice(lc * LANES, (lc + 1) * LANES)
                cnt = [rows16(cnt_ref, ch, h, g) for h in range(PEER_HEADS)]
                ga = [rows16(ga_ref, ch, h, g) for h in range(PEER_HEADS)]
                for sl in range(N_KEYS // PEER_SLAB):
                    rs = slice(sl * SUBLANES, (sl + 1) * SUBLANES)
                    es = slice(gg * N_KEYS + sl * PEER_SLAB, gg * N_KEYS + (sl + 1) * PEER_SLAB)
                    w = None
                    for h in range(PEER_HEADS):
                        rank = pltpu.bitcast(rank_ref[ch, h, rs, :], BF16)
                        gb = pltpu.bitcast(gb_ref[ch, h, rs, :], BF16)
                        term = jnp.where(rank < cnt[h], ga[h], 0.0) * gb
                        w = term if w is None else w + term
                    wg_sc[p][es, ls] = w * _gelu(act_sc[p][es, ls].astype(BF16))

    stage_a(0, 0)
    stage_a(1, 1)
    stage_b(0, 0)

    def steady(k, carry):
        i = 2 * k
        stage_a(i + 2, 0)
        stage_b(i + 1, 1)
        stage_c(i, 0)
        stage_a(i + 3, 1)
        stage_b(i + 2, 0)
        stage_c(i + 1, 1)
        return carry

    lax.fori_loop(0, n_stage // 2 - 1, steady, 0)
    stage_b(n_stage - 1, 1)
    stage_c(n_stage - 2, 0)
    stage_c(n_stage - 1, 1)

    @pl.when(j == pl.num_programs(1) - 1)
    def _():
        for t in range(n_part):
            ts = slice(t * PEER_TOK, (t + 1) * PEER_TOK)
            y = DN_ALPHA * x_ref[ts, :] + acc_sc[t].T
            o_ref[ts, :] = _layer_norm(y, g_ref[0:1, :], b_ref[0:1, :])


def _pack_body(w_ref, o_ref, *, transpose):
    w = w_ref[...]
    if transpose:
        w = w.T
    o_ref[0] = pltpu.bitcast(w.astype(BF16), jnp.uint32)


def _pack_experts(w, transpose):
    n_exp, D = w.shape
    rows, cols = (D, PEER_SUB) if transpose else (PEER_SUB, D)
    return pl.pallas_call(
        functools.partial(_pack_body, transpose=transpose),
        grid=(n_exp // PEER_SUB,),
        in_specs=[pl.BlockSpec((PEER_SUB, D), lambda i: (i, 0))],
        out_specs=pl.BlockSpec((1, rows // 2, cols), lambda i: (i, 0, 0)),
        out_shape=jax.ShapeDtypeStruct((n_exp // PEER_SUB, rows // 2, cols), jnp.uint32),
        compiler_params=pltpu.CompilerParams(dimension_semantics=("arbitrary",)),
        name="pack",
    )(w)


def _peer(x1, cnt, ga, rank2, gb, eu3, evt3, ln_g, ln_b):
    T, D = x1.shape
    n_exp = eu3.shape[0] * PEER_SUB
    tc = PEER_TOK_PARTS * PEER_TOK
    if T % tc:
        tc = 2 * PEER_TOK
    assert T % tc == 0
    ns = PEER_SUBS_PER_STEP
    ne = ns * PEER_SUB
    gs = ne // N_KEYS
    nch = tc // LANES
    per_group = pl.BlockSpec((nch, PEER_HEADS, gs, LANES), lambda i, j: (i, 0, j, 0))
    per_key = pl.BlockSpec((nch, PEER_HEADS, N_KEYS // 2, LANES), lambda i, j: (i, 0, 0, 0))
    const = lambda i, j: (0, 0)
    rows = lambda v: jnp.broadcast_to(v.reshape(1, D), (SUBLANES, D))
    return pl.pallas_call(
        _peer_body,
        grid=(T // tc, n_exp // ne),
        in_specs=[
            pl.BlockSpec((tc, D), lambda i, j: (i, 0)),
            per_group, per_group, per_key, per_key,
            pl.BlockSpec((ns, PEER_SUB // 2, D), lambda i, j: (j, 0, 0)),
            pl.BlockSpec((ns, D // 2, PEER_SUB), lambda i, j: (j, 0, 0)),
            pl.BlockSpec((SUBLANES, D), const),
            pl.BlockSpec((SUBLANES, D), const),
        ],
        out_specs=pl.BlockSpec((tc, D), lambda i, j: (i, 0)),
        out_shape=jax.ShapeDtypeStruct((T, D), F32),
        scratch_shapes=[
            pltpu.VMEM((tc // PEER_TOK, PEER_TOK, D), BF16),
            pltpu.VMEM((PEER_SUB, PEER_TOK), F32),
            pltpu.VMEM((PEER_SUB, PEER_TOK), F32),
            pltpu.VMEM((PEER_SUB, PEER_TOK), BF16),
            pltpu.VMEM((PEER_SUB, PEER_TOK), BF16),
            pltpu.VMEM((tc // PEER_TOK, D, PEER_TOK), F32),
        ],
        compiler_params=pltpu.CompilerParams(
            dimension_semantics=("arbitrary", "arbitrary")),
        name="peer",
    )(x1, cnt, ga, rank2, gb, eu3, evt3, rows(ln_g), rows(ln_b))


def _prepare_layer(l, w_in, lambda_qk, subln_g, conv_w, conv_b, gate_a_w, gate_a_b, gate_i_w,
                   gate_i_b, lru_lambda, w_out, ln1_g, ln1_b, peer_wq, peer_keys, expert_u,
                   expert_v, ln2_g, ln2_b):
    gate_w = jnp.stack([
        jnp.concatenate([_block_diag(gate_a_w[l, d]), _block_diag(gate_i_w[l, d])], axis=1)
        for d in range(2)]).astype(BF16)
    return dict(
        lambda_init=0.8 - 0.6 * math.exp(-0.3 * l),
        w_in=w_in[l].astype(BF16), lambda_qk=lambda_qk[l], subln_g=subln_g[l],
        conv_w=conv_w[l], conv_b=conv_b[l], gate_w=gate_w,
        gate_b=jnp.concatenate([gate_a_b[l], gate_i_b[l]], axis=-1)[:, None, :],
        lru_lambda=lru_lambda[l][:, None, :], w_out=w_out[l].astype(BF16),
        ln1_g=ln1_g[l], ln1_b=ln1_b[l], peer_wq=peer_wq[l].astype(BF16),
        peer_keys=peer_keys[l].astype(BF16), eu=_pack_experts(expert_u[l], False),
        evt=_pack_experts(expert_v[l], True), ln2_g=ln2_g[l], ln2_b=ln2_b[l])


def _layer(x, p, tables):
    B, S, D = x.shape
    q, k, v, xr, gate = _proj(x, p["w_in"], *tables)
    attn = _attention(q, k, v, p["lambda_qk"], p["subln_g"], p["lambda_init"])
    h = _lru(xr, p["conv_w"], p["conv_b"], p["gate_w"], p["gate_b"], p["lru_lambda"])
    x1 = _outproj(x, attn, h, gate, p["w_out"], p["ln1_g"], p["ln1_b"]).reshape(B * S, D)
    cnt, ga, rank2, gb = _route(x1, p["peer_wq"], p["peer_keys"])
    return _peer(x1, cnt, ga, rank2, gb, p["eu"], p["evt"], p["ln2_g"], p["ln2_b"]).reshape(B, S, D)


def _trunks(xs, *weights):
    layers = [_prepare_layer(l, *weights) for l in range(DEPTH)]
    outs = []
    for x in xs:
        tables = _rotary_tables(x.shape[1])
        for p in layers:
            x = _layer(x, p, tables)
        outs.append(x)
    return tuple(outs)


def _trunk(x, *weights):
    return _trunks((x,), *weights)[0]


def kernel(x_prompt, x_sample, w_in, lambda_qk, subln_g, conv_w, conv_b, gate_a_w, gate_a_b,
           gate_i_w, gate_i_b, lru_lambda, w_out, ln1_g, ln1_b, peer_wq, peer_keys, expert_u,
           expert_v, ln2_g, ln2_b):
    return _trunks((x_prompt, x_sample), w_in, lambda_qk, subln_g, conv_w, conv_b, gate_a_w,
                   gate_a_b, gate_i_w, gate_i_b, lru_lambda, w_out, ln1_g, ln1_b, peer_wq,
                   peer_keys, expert_u, expert_v, ln2_g, ln2_b)
```

```python
import functools
import math

import jax
import jax.numpy as jnp
from jax import lax
from jax.experimental import pallas as pl
from jax.experimental.pallas import tpu as pltpu

F32 = jnp.float32
BF16 = jnp.bfloat16

D_MODEL = 1024
DEPTH = 2
ATTN_WIDTH = 512
LRU_WIDTH = 512
N_DIFF_HEADS = 4
QK_HEAD_DIM = 64
V_HEAD_DIM = 128
ROT_DIM = 16
ROPE_THETA = 500000.0
CONV_LEFT = 2
LRU_BLOCKS = 8
LRU_BLOCK_DIM = 64
LRU_C = 8.0
PEER_HEADS = 8
N_KEYS = 128
PEER_TOPK = 16
PEER_HALF = 128
IN_WIDTH = 3 * ATTN_WIDTH + 2 * LRU_WIDTH
DN_ALPHA = (2 * DEPTH) ** 0.25
LN_EPS = 1e-5
SUBLN_EPS = 1e-5

LANES = 128
SUBLANES = 8
NEG_INF = float("-inf")
LOG2_E = math.log2(math.e)

_NT = (((1,), (1,)), ((), ()))


def _tile(n, want):
    t = min(n, want)
    assert n % t == 0, (n, t)
    return t


def _layer_norm(y, g, b):
    mu = jnp.mean(y, axis=-1, keepdims=True)
    yc = y - mu
    var = jnp.mean(yc * yc, axis=-1, keepdims=True)
    return yc * lax.rsqrt(var + LN_EPS) * g + b


def _gelu(x):
    c = math.sqrt(2.0 / math.pi)
    hx = 0.5 * x
    return hx + hx * jnp.tanh(x * (c + (c * 0.044715) * (x * x)))


def _proj_body(x_ref, w_ref, cos_ref, sa_ref, sb_ref, q_ref, k_ref, v_ref, xr_ref, gate_ref):
    xb = x_ref[0].astype(BF16)
    cos = cos_ref[...]
    sa = sa_ref[...]
    sb = sb_ref[...]

    def rotate(t):
        return t * cos + pltpu.roll(t, 8, 1) * sa + pltpu.roll(t, LANES - 8, 1) * sb

    def proj(c0, width):
        return jnp.dot(xb, w_ref[:, c0:c0 + width], preferred_element_type=F32)

    q = proj(0, ATTN_WIDTH)
    k = proj(ATTN_WIDTH, ATTN_WIDTH)
    for h in range(N_DIFF_HEADS):
        cs = slice(h * LANES, (h + 1) * LANES)
        q_ref[0, :, cs] = (rotate(q[:, cs]) * (QK_HEAD_DIM ** -0.5 * LOG2_E)).astype(BF16)
        k_ref[0, :, cs] = rotate(k[:, cs]).astype(BF16)
    v_ref[0] = proj(2 * ATTN_WIDTH, ATTN_WIDTH).astype(BF16)
    xr_ref[0] = proj(3 * ATTN_WIDTH, LRU_WIDTH)
    gate_ref[0] = proj(3 * ATTN_WIDTH + LRU_WIDTH, LRU_WIDTH)


def _proj(x, w_in, cos_t, sa_t, sb_t):
    B, S, D = x.shape
    tm = _tile(S, 512)
    row = lambda b, i: (b, i, 0)
    tab = lambda b, i: (i, 0)
    out_bf = jax.ShapeDtypeStruct((B, S, ATTN_WIDTH), BF16)
    out_f = jax.ShapeDtypeStruct((B, S, LRU_WIDTH), F32)
    return pl.pallas_call(
        _proj_body,
        grid=(B, S // tm),
        in_specs=[
            pl.BlockSpec((1, tm, D), row),
            pl.BlockSpec((D, IN_WIDTH), lambda b, i: (0, 0)),
            pl.BlockSpec((tm, LANES), tab),
            pl.BlockSpec((tm, LANES), tab),
            pl.BlockSpec((tm, LANES), tab),
        ],
        out_specs=[pl.BlockSpec((1, tm, ATTN_WIDTH), row)] * 5,
        out_shape=[out_bf, out_bf, out_bf, out_f, out_f],
        compiler_params=pltpu.CompilerParams(
            dimension_semantics=("arbitrary", "arbitrary")),
        name="proj",
    )(x, w_in, cos_t, sa_t, sb_t)


def _rotary_tables(S):
    half = ROT_DIM // 2
    inv = ROPE_THETA ** (-jnp.arange(0, ROT_DIM, 2, dtype=F32) / ROT_DIM)
    ang = jnp.arange(S, dtype=F32)[:, None] * inv[None, :]
    cos, sin = jnp.cos(ang), jnp.sin(ang)
    ones = jnp.ones((S, QK_HEAD_DIM - ROT_DIM), F32)
    zeros = jnp.zeros((S, QK_HEAD_DIM - ROT_DIM), F32)
    z8 = jnp.zeros((S, half), F32)
    cos_t = jnp.concatenate([cos, cos, ones], axis=-1)
    sa_t = jnp.concatenate([z8, sin, zeros], axis=-1)
    sb_t = jnp.concatenate([-sin, z8, zeros], axis=-1)
    rep = lambda t: jnp.concatenate([t, t], axis=-1)
    return rep(cos_t), rep(sa_t), rep(sb_t)


def _attn_body(lam_ref, g_ref, q_ref, k_ref, v_ref, o_ref, km_sc, va_sc, m_sc, acc_sc, *,
               lambda_init, tk):
    seq = k_ref.shape[1]

    @pl.when(pl.program_id(2) == 0)
    def _():
        k = k_ref[0]
        lane = lax.broadcasted_iota(jnp.int32, k.shape, 1)
        zero = jnp.zeros_like(k)
        km_sc[0] = jnp.where(lane < QK_HEAD_DIM, k, zero)
        km_sc[1] = jnp.where(lane >= QK_HEAD_DIM, k, zero)
        va_sc[:, :V_HEAD_DIM] = v_ref[0]
        va_sc[:, V_HEAD_DIM:] = jnp.ones((seq, V_HEAD_DIM), BF16)

    m_sc[...] = jnp.full(m_sc.shape, NEG_INF, F32)
    acc_sc[...] = jnp.zeros(acc_sc.shape, F32)
    q = q_ref[0]

    def chunk(j, carry):
        ks = pl.ds(pl.multiple_of(j * tk, tk), tk)
        for c in range(2):
            s = lax.dot_general(q, km_sc[c, ks, :], _NT, preferred_element_type=F32)
            m_prev = m_sc[c]
            m_new = jnp.maximum(m_prev, jnp.max(s, axis=1, keepdims=True))
            alpha = jnp.exp2(m_prev - m_new)
            p = jnp.exp2(s - jnp.tile(m_new, (1, tk // LANES)))
            pv = jnp.dot(p.astype(BF16), va_sc[ks, :], preferred_element_type=F32)
            acc_sc[c] = jnp.tile(alpha, (1, 2)) * acc_sc[c] + pv
            m_sc[c] = m_new
        return carry

    lax.fori_loop(0, seq // tk, chunk, 0, unroll=True)

    lp = lam_ref[...]
    d1 = jnp.sum(lp[0:1] * lp[1:2], axis=1, keepdims=True)
    d2 = jnp.sum(lp[2:3] * lp[3:4], axis=1, keepdims=True)
    lam = jnp.exp(d1) - jnp.exp(d2) + lambda_init
    acc1 = acc_sc[0]
    acc2 = acc_sc[1]
    o = (acc1[:, :V_HEAD_DIM] / acc1[:, V_HEAD_DIM:]
         - lam * (acc2[:, :V_HEAD_DIM] / acc2[:, V_HEAD_DIM:]))
    o = o * lax.rsqrt(jnp.mean(o * o, axis=1, keepdims=True) + SUBLN_EPS) * g_ref[...]
    o_ref[0] = (o * (1.0 - lambda_init)).astype(o_ref.dtype)


def _attention(q, k, v, lambda_qk, subln_g, lambda_init):
    B, S, _ = q.shape
    tq = _tile(S, 1024)
    tk = _tile(S, 512)
    return pl.pallas_call(
        functools.partial(_attn_body, lambda_init=lambda_init, tk=tk),
        grid=(B, N_DIFF_HEADS, S // tq),
        in_specs=[
            pl.BlockSpec((4, QK_HEAD_DIM), lambda b, h, i: (0, 0)),
            pl.BlockSpec((1, V_HEAD_DIM), lambda b, h, i: (0, 0)),
            pl.BlockSpec((1, tq, LANES), lambda b, h, i: (b, i, h)),
            pl.BlockSpec((1, S, LANES), lambda b, h, i: (b, 0, h)),
            pl.BlockSpec((1, S, LANES), lambda b, h, i: (b, 0, h)),
        ],
        out_specs=pl.BlockSpec((1, tq, LANES), lambda b, h, i: (b, i, h)),
        out_shape=jax.ShapeDtypeStruct((B, S, ATTN_WIDTH), BF16),
        scratch_shapes=[
            pltpu.VMEM((2, S, LANES), BF16),
            pltpu.VMEM((S, 2 * V_HEAD_DIM), BF16),
            pltpu.VMEM((2, tq, LANES), F32),
            pltpu.VMEM((2, tq, 2 * V_HEAD_DIM), F32),
        ],
        compiler_params=pltpu.CompilerParams(
            dimension_semantics=("arbitrary", "arbitrary", "arbitrary")),
        name="attn",
    )(lambda_qk, subln_g.reshape(1, V_HEAD_DIM), q, k, v)


def _lru_body(x_ref, prev_ref, next_ref, cw_ref, cb_ref, gw_ref, gb_ref, lam_ref, h_ref,
              a_sc, b_sc, carry_sc):
    d = pl.program_id(1)
    j = pl.program_id(2)
    n_t = pl.num_programs(2)
    tt = x_ref.shape[1]
    jj = jnp.where(d == 0, j, n_t - 1 - j)

    @pl.when(j == 0)
    def _():
        carry_sc[...] = jnp.zeros(carry_sc.shape, F32)

    x = x_ref[0]
    prev = prev_ref[0] * (jj > 0).astype(F32)
    nxt = next_ref[0] * (jj < n_t - 1).astype(F32)
    row = lax.broadcasted_iota(jnp.int32, x.shape, 0)
    cw = cw_ref[...]
    x_m1 = jnp.where(row == 0, prev[7:8], pltpu.roll(x, 1, 0))
    x_m2 = jnp.where(row == 0, prev[6:7], jnp.where(row == 1, prev[7:8], pltpu.roll(x, 2, 0)))
    x_p1 = jnp.where(row == tt - 1, nxt[0:1], pltpu.roll(x, tt - 1, 0))
    xc = x_m2 * cw[0:1] + x_m1 * cw[1:2] + x * cw[2:3] + x_p1 * cw[3:4] + cb_ref[...]

    gates = jnp.dot(xc.astype(BF16), gw_ref[0], preferred_element_type=F32) + gb_ref[0]
    r = jax.nn.sigmoid(gates[:, :LRU_WIDTH])
    i = jax.nn.sigmoid(gates[:, LRU_WIDTH:])
    lam = lam_ref[0]
    softplus_neg = jnp.maximum(-lam, 0.0) + jnp.log(1.0 + jnp.exp(-jnp.abs(lam)))
    log_a = -LRU_C * r * softplus_neg
    a = jnp.exp(log_a)
    b = jnp.sqrt(1.0 - a * a) * (i * xc)

    sub = row & (SUBLANES - 1)

    def local_scan(a, b, reverse):
        shift = 1
        while shift < SUBLANES:
            if reverse:
                a_s = pltpu.roll(a, tt - shift, 0)
                b_s = pltpu.roll(b, tt - shift, 0)
                valid = sub < SUBLANES - shift
            else:
                a_s = pltpu.roll(a, shift, 0)
                b_s = pltpu.roll(b, shift, 0)
                valid = sub >= shift
            b = a * jnp.where(valid, b_s, 0.0) + b
            a = a * jnp.where(valid, a_s, 1.0)
            shift *= 2
        return a, b

    n_groups = tt // SUBLANES

    def sweep(reverse):
        a_l, b_l = local_scan(a, b, reverse)
        a_sc[...] = a_l
        b_sc[...] = b_l

        def step(g, carry):
            gi = (n_groups - 1 - g) if reverse else g
            rs = pl.ds(pl.multiple_of(gi * SUBLANES, SUBLANES), SUBLANES)
            h = b_sc[rs, :] + a_sc[rs, :] * carry
            h_ref[0, 0, rs, :] = h
            return h[0:1] if reverse else h[SUBLANES - 1:SUBLANES]

        carry_sc[...] = lax.fori_loop(0, n_groups, step, carry_sc[...])

    @pl.when(d == 0)
    def _():
        sweep(False)

    @pl.when(d == 1)
    def _():
        sweep(True)


def _lru(xr, conv_w, conv_b, gate_w, gate_b, lru_lam):
    B, S, W = xr.shape
    tt = _tile(S, 1024)
    n_t = S // tt
    hb = tt // SUBLANES
    n_hb = S // SUBLANES

    def tblk(d, j):
        return jnp.where(d == 0, j, n_t - 1 - j)

    return pl.pallas_call(
        _lru_body,
        grid=(B, 2, n_t),
        in_specs=[
            pl.BlockSpec((1, tt, W), lambda b, d, j: (b, tblk(d, j), 0)),
            pl.BlockSpec((1, SUBLANES, W),
                         lambda b, d, j: (b, jnp.maximum(tblk(d, j) * hb - 1, 0), 0)),
            pl.BlockSpec((1, SUBLANES, W),
                         lambda b, d, j: (b, jnp.minimum((tblk(d, j) + 1) * hb, n_hb - 1), 0)),
            pl.BlockSpec((4, W), lambda b, d, j: (0, 0)),
            pl.BlockSpec((1, W), lambda b, d, j: (0, 0)),
            pl.BlockSpec((1, W, 2 * W), lambda b, d, j: (d, 0, 0)),
            pl.BlockSpec((1, 1, 2 * W), lambda b, d, j: (d, 0, 0)),
            pl.BlockSpec((1, 1, W), lambda b, d, j: (d, 0, 0)),
        ],
        out_specs=pl.BlockSpec((1, 1, tt, W), lambda b, d, j: (b, d, tblk(d, j), 0)),
        out_shape=jax.ShapeDtypeStruct((B, 2, S, W), F32),
        scratch_shapes=[
            pltpu.VMEM((tt, W), F32),
            pltpu.VMEM((tt, W), F32),
            pltpu.VMEM((1, W), F32),
        ],
        compiler_params=pltpu.CompilerParams(
            dimension_semantics=("arbitrary", "arbitrary", "arbitrary")),
        name="lru",
    )(xr, xr, xr, conv_w, conv_b.reshape(1, W), gate_w, gate_b, lru_lam)


def _block_diag(w):
    eye = jnp.eye(LRU_BLOCKS, dtype=w.dtype)
    return jnp.einsum('nde,nm->ndme', w, eye).reshape(LRU_WIDTH, LRU_WIDTH)


def _outproj_body(x_ref, attn_ref, hf_ref, hb_ref, gate_ref, wo_ref, g_ref, b_ref, o_ref):
    rec = (hf_ref[0, 0] + hb_ref[0, 0]) * _gelu(gate_ref[0])
    mix = jnp.dot(attn_ref[0], wo_ref[:ATTN_WIDTH], preferred_element_type=F32)
    mix += jnp.dot(rec.astype(BF16), wo_ref[ATTN_WIDTH:], preferred_element_type=F32)
    o_ref[0] = _layer_norm(DN_ALPHA * x_ref[0] + mix, g_ref[...], b_ref[...])


def _outproj(x, attn, h, gate, w_out, ln_g, ln_b):
    B, S, D = x.shape
    tm = _tile(S, 512)
    row = lambda b, i: (b, i, 0)
    const = lambda b, i: (0, 0)
    return pl.pallas_call(
        _outproj_body,
        grid=(B, S // tm),
        in_specs=[
            pl.BlockSpec((1, tm, D), row),
            pl.BlockSpec((1, tm, ATTN_WIDTH), row),
            pl.BlockSpec((1, 1, tm, LRU_WIDTH), lambda b, i: (b, 0, i, 0)),
            pl.BlockSpec((1, 1, tm, LRU_WIDTH), lambda b, i: (b, 1, i, 0)),
            pl.BlockSpec((1, tm, LRU_WIDTH), row),
            pl.BlockSpec((ATTN_WIDTH + LRU_WIDTH, D), const),
            pl.BlockSpec((1, D), const),
            pl.BlockSpec((1, D), const),
        ],
        out_specs=pl.BlockSpec((1, tm, D), row),
        out_shape=jax.ShapeDtypeStruct((B, S, D), F32),
        compiler_params=pltpu.CompilerParams(
            dimension_semantics=("arbitrary", "arbitrary")),
        name="outproj",
    )(x, attn, h, h, gate, w_out, ln_g.reshape(1, D), ln_b.reshape(1, D))


_CAND_PAIRS = [(i, j) for i in range(PEER_TOPK + 1) for j in range(PEER_TOPK + 1)
               if (i + 1) * (j + 1) <= PEER_TOPK + 1]
_CAND_ROWS = -(-len(_CAND_PAIRS) // SUBLANES) * SUBLANES


def _batcher_pairs(n):
    def merge(lo, hi, r):
        step = 2 * r
        if step < hi - lo:
            yield from merge(lo, hi, step)
            yield from merge(lo + r, hi, step)
            yield from ((i, i + r) for i in range(lo + r, hi - r, step))
        else:
            yield (lo, lo + r)

    def sort(lo, hi):
        if hi - lo >= 1:
            mid = lo + (hi - lo) // 2
            yield from sort(lo, mid)
            yield from sort(mid + 1, hi)
            yield from merge(lo, hi, 1)

    return list(sort(0, n - 1))


def _top_values(s, n):
    v = [s[i * SUBLANES:(i + 1) * SUBLANES, :] for i in range(s.shape[0] // SUBLANES)]
    for i, j in _batcher_pairs(len(v)):
        v[i], v[j] = jnp.maximum(v[i], v[j]), jnp.minimum(v[i], v[j])
    out = []
    for r in range(n):
        mx = jnp.max(v[0], axis=0, keepdims=True)
        out.append(mx)
        depth = min(len(v), n - 1 - r)
        if depth:
            hit = v[0] == mx
            for i in range(depth):
                below = v[i + 1] if i + 1 < len(v) else NEG_INF
                v[i] = jnp.where(hit, below, v[i])
    return out


def _dup_bf16(x):
    u = pltpu.bitcast(x.astype(BF16).astype(F32), jnp.uint32)
    return u | lax.shift_right_logical(u, jnp.uint32(16))


def _route_body(x_ref, wq_ref, keys_ref, cnt_ref, ga_ref, rank_ref, gb_ref, cand_sc):
    xb = x_ref[...].astype(BF16)
    qp = jnp.dot(xb, wq_ref[...], preferred_element_type=F32)
    cand_sc[...] = jnp.full(cand_sc.shape, NEG_INF, F32)
    n = PEER_TOPK + 1
    for h in range(PEER_HEADS):
        s = []
        for p in range(2):
            c0 = (2 * h + p) * PEER_HALF
            blk = qp[:, c0:c0 + PEER_HALF].astype(BF16)
            s.append(lax.dot_general(keys_ref[p], blk, _NT, preferred_element_type=F32))
        s1, s2 = s
        a = _top_values(s1, n)
        b = _top_values(s2, n)
        rank2 = jnp.full(s2.shape, float(n), F32)
        for j in reversed(range(n)):
            rank2 = jnp.where(s2 >= b[j], float(j), rank2)
        for r, (i, j) in enumerate(_CAND_PAIRS):
            cand_sc[r:r + 1, :] = a[i] + b[j]
        cand = cand_sc[...]
        cur = cand
        for r in range(PEER_TOPK):
            mx = jnp.max(cur, axis=0, keepdims=True)
            cur = jnp.where(cur == mx, NEG_INF, cur)
        thr = 0.5 * (mx + jnp.max(cur, axis=0, keepdims=True))
        z = jnp.sum(jnp.where(cand >= thr, jnp.exp(cand - (a[0] + b[0])), 0.0),
                    axis=0, keepdims=True)
        need = thr - s1
        cnt = jnp.zeros(s1.shape, F32)
        for j in range(n):
            cnt = jnp.where(b[j] >= need, j + 1.0, cnt)
        outs = (
            (cnt_ref, _dup_bf16(cnt)),
            (ga_ref, _dup_bf16(jnp.exp(s1 - a[0]) / z)),
            (rank_ref, pltpu.bitcast(rank2.astype(BF16), jnp.uint32)),
            (gb_ref, pltpu.bitcast(jnp.exp(s2 - b[0]).astype(BF16), jnp.uint32)),
        )
        for ref, val in outs:
            for c in range(val.shape[1] // LANES):
                ref[c, h] = val[:, c * LANES:(c + 1) * LANES]


def _route(x1, wq, keys):
    T, D = x1.shape
    tm = _tile(T, 256)
    shape = (T // LANES, PEER_HEADS, N_KEYS, LANES)
    packed = (T // LANES, PEER_HEADS, N_KEYS // 2, LANES)
    ospec = pl.BlockSpec((tm // LANES,) + shape[1:], lambda i: (i, 0, 0, 0))
    pspec = pl.BlockSpec((tm // LANES,) + packed[1:], lambda i: (i, 0, 0, 0))
    return pl.pallas_call(
        _route_body,
        grid=(T // tm,),
        in_specs=[
            pl.BlockSpec((tm, D), lambda i: (i, 0)),
            pl.BlockSpec((D, 2 * PEER_HEADS * PEER_HALF), lambda i: (0, 0)),
            pl.BlockSpec((2, N_KEYS, PEER_HALF), lambda i: (0, 0, 0)),
        ],
        out_specs=[ospec, ospec, pspec, pspec],
        out_shape=[jax.ShapeDtypeStruct(shape, jnp.uint32), jax.ShapeDtypeStruct(shape, jnp.uint32),
                   jax.ShapeDtypeStruct(packed, jnp.uint32), jax.ShapeDtypeStruct(packed, jnp.uint32)],
        scratch_shapes=[pltpu.VMEM((_CAND_ROWS, tm), F32)],
        compiler_params=pltpu.CompilerParams(dimension_semantics=("arbitrary",)),
        name="route",
    )(x1, wq, keys)


PEER_SUB = 1024
PEER_SUBS_PER_STEP = 2
PEER_TOK = 256
PEER_TOK_PARTS = 4
PEER_SLAB = 16


def _peer_body(x_ref, cnt_ref, ga_ref, rank_ref, gb_ref, eu_ref, evt_ref, g_ref, b_ref, o_ref,
               xb_sc, act0_sc, act1_sc, wg0_sc, wg1_sc, acc_sc):
    j = pl.program_id(1)
    n_part = xb_sc.shape[0]
    n_stage = PEER_SUBS_PER_STEP * n_part
    assert n_part % 2 == 0 and n_part & (n_part - 1) == 0
    part_bits = n_part.bit_length() - 1
    act_sc = (act0_sc, act1_sc)
    wg_sc = (wg0_sc, wg1_sc)

    @pl.when(j == 0)
    def _():
        for t in range(n_part):
            xb_sc[t] = x_ref[t * PEER_TOK:(t + 1) * PEER_TOK, :].astype(BF16)
        acc_sc[...] = jnp.zeros(acc_sc.shape, F32)

    def split(i):
        if isinstance(i, int):
            return i // n_part, i % n_part
        return lax.shift_right_logical(i, part_bits), i & (n_part - 1)

    def stage_a(i, p):
        s, t = split(i)
        eu = pltpu.bitcast(eu_ref[s], BF16)
        act_sc[p][...] = lax.dot_general(eu, xb_sc[t], _NT, preferred_element_type=F32)

    def stage_c(i, p):
        s, t = split(i)
        evt = pltpu.bitcast(evt_ref[s], BF16)
        acc_sc[t] += jnp.dot(evt, wg_sc[p][...], preferred_element_type=F32)

    def rows16(ref, ch, h, g):
        row = ref[ch, h, pl.ds(g, 1), :]
        return pltpu.bitcast(jnp.broadcast_to(row, (SUBLANES, LANES)), BF16)

    def stage_b(i, p):
        s, t = split(i)
        for gg in range(PEER_SUB // N_KEYS):
            g = s * (PEER_SUB // N_KEYS) + gg
            for lc in range(PEER_TOK // LANES):
                ch = t * (PEER_TOK // LANES) + lc
                ls = slice(lc * LANES, (lc + 1) * LANES)
                cnt = [rows16(cnt_ref, ch, h, g) for h in range(PEER_HEADS)]
                ga = [rows16(ga_ref, ch, h, g) for h in range(PEER_HEADS)]
                for sl in range(N_KEYS // PEER_SLAB):
                    rs = slice(sl * SUBLANES, (sl + 1) * SUBLANES)
                    es = slice(gg * N_KEYS + sl * PEER_SLAB, gg * N_KEYS + (sl + 1) * PEER_SLAB)
                    w = None
                    for h in range(PEER_HEADS):
                        rank = pltpu.bitcast(rank_ref[ch, h, rs, :], BF16)
                        gb = pltpu.bitcast(gb_ref[ch, h, rs, :], BF16)
                        term = jnp.where(rank < cnt[h], ga[h], 0.0) * gb
                        w = term if w is None else w + term
                    wg_sc[p][es, ls] = w * _gelu(act_sc[p][es, ls].astype(BF16))

    stage_a(0, 0)
    stage_a(1, 1)
    stage_b(0, 0)

    def steady(k, carry):
        i = 2 * k
        stage_a(i + 2, 0)
        stage_b(i + 1, 1)
        stage_c(i, 0)
        stage_a(i + 3, 1)
        stage_b(i + 2, 0)
        stage_c(i + 1, 1)
        return carry

    lax.fori_loop(0, n_stage // 2 - 1, steady, 0)
    stage_b(n_stage - 1, 1)
    stage_c(n_stage - 2, 0)
    stage_c(n_stage - 1, 1)

    @pl.when(j == pl.num_programs(1) - 1)
    def _():
        for t in range(n_part):
            ts = slice(t * PEER_TOK, (t + 1) * PEER_TOK)
            y = DN_ALPHA * x_ref[ts, :] + acc_sc[t].T
            o_ref[ts, :] = _layer_norm(y, g_ref[...], b_ref[...])


def _pack_body(w_ref, o_ref, *, transpose):
    w = w_ref[...]
    if transpose:
        w = w.T
    o_ref[0] = pltpu.bitcast(w.astype(BF16), jnp.uint32)


def _pack_experts(w, transpose):
    n_exp, D = w.shape
    rows, cols = (D, PEER_SUB) if transpose else (PEER_SUB, D)
    return pl.pallas_call(
        functools.partial(_pack_body, transpose=transpose),
        grid=(n_exp // PEER_SUB,),
        in_specs=[pl.BlockSpec((PEER_SUB, D), lambda i: (i, 0))],
        out_specs=pl.BlockSpec((1, rows // 2, cols), lambda i: (i, 0, 0)),
        out_shape=jax.ShapeDtypeStruct((n_exp // PEER_SUB, rows // 2, cols), jnp.uint32),
        compiler_params=pltpu.CompilerParams(dimension_semantics=("arbitrary",)),
        name="pack",
    )(w)


def _peer(x1, cnt, ga, rank2, gb, eu3, evt3, ln_g, ln_b):
    T, D = x1.shape
    n_exp = eu3.shape[0] * PEER_SUB
    tc = PEER_TOK_PARTS * PEER_TOK
    if T % tc:
        tc = 2 * PEER_TOK
    assert T % tc == 0
    ns = PEER_SUBS_PER_STEP
    ne = ns * PEER_SUB
    gs = ne // N_KEYS
    nch = tc // LANES
    per_group = pl.BlockSpec((nch, PEER_HEADS, gs, LANES), lambda i, j: (i, 0, j, 0))
    per_key = pl.BlockSpec((nch, PEER_HEADS, N_KEYS // 2, LANES), lambda i, j: (i, 0, 0, 0))
    const = lambda i, j: (0, 0)
    return pl.pallas_call(
        _peer_body,
        grid=(T // tc, n_exp // ne),
        in_specs=[
            pl.BlockSpec((tc, D), lambda i, j: (i, 0)),
            per_group, per_group, per_key, per_key,
            pl.BlockSpec((ns, PEER_SUB // 2, D), lambda i, j: (j, 0, 0)),
            pl.BlockSpec((ns, D // 2, PEER_SUB), lambda i, j: (j, 0, 0)),
            pl.BlockSpec((1, D), const),
            pl.BlockSpec((1, D), const),
        ],
        out_specs=pl.BlockSpec((tc, D), lambda i, j: (i, 0)),
        out_shape=jax.ShapeDtypeStruct((T, D), F32),
        scratch_shapes=[
            pltpu.VMEM((tc // PEER_TOK, PEER_TOK, D), BF16),
            pltpu.VMEM((PEER_SUB, PEER_TOK), F32),
            pltpu.VMEM((PEER_SUB, PEER_TOK), F32),
            pltpu.VMEM((PEER_SUB, PEER_TOK), BF16),
            pltpu.VMEM((PEER_SUB, PEER_TOK), BF16),
            pltpu.VMEM((tc // PEER_TOK, D, PEER_TOK), F32),
        ],
        compiler_params=pltpu.CompilerParams(
            dimension_semantics=("arbitrary", "arbitrary")),
        name="peer",
    )(x1, cnt, ga, rank2, gb, eu3, evt3, ln_g.reshape(1, D), ln_b.reshape(1, D))


def _prepare_layer(l, w_in, lambda_qk, subln_g, conv_w, conv_b, gate_a_w, gate_a_b, gate_i_w,
                   gate_i_b, lru_lambda, w_out, ln1_g, ln1_b, peer_wq, peer_keys, expert_u,
                   expert_v, ln2_g, ln2_b):
    gate_w = jnp.stack([
        jnp.concatenate([_block_diag(gate_a_w[l, d]), _block_diag(gate_i_w[l, d])], axis=1)
        for d in range(2)]).astype(BF16)
    return dict(
        lambda_init=0.8 - 0.6 * math.exp(-0.3 * l),
        w_in=w_in[l].astype(BF16), lambda_qk=lambda_qk[l], subln_g=subln_g[l],
        conv_w=conv_w[l], conv_b=conv_b[l], gate_w=gate_w,
        gate_b=jnp.concatenate([gate_a_b[l], gate_i_b[l]], axis=-1)[:, None, :],
        lru_lambda=lru_lambda[l][:, None, :], w_out=w_out[l].astype(BF16),
        ln1_g=ln1_g[l], ln1_b=ln1_b[l], peer_wq=peer_wq[l].astype(BF16),
        peer_keys=peer_keys[l].astype(BF16), eu=_pack_experts(expert_u[l], False),
        evt=_pack_experts(expert_v[l], True), ln2_g=ln2_g[l], ln2_b=ln2_b[l])


def _layer(x, p, tables):
    B, S, D = x.shape
    q, k, v, xr, gate = _proj(x, p["w_in"], *tables)
    attn = _attention(q, k, v, p["lambda_qk"], p["subln_g"], p["lambda_init"])
    h = _lru(xr, p["conv_w"], p["conv_b"], p["gate_w"], p["gate_b"], p["lru_lambda"])
    x1 = _outproj(x, attn, h, gate, p["w_out"], p["ln1_g"], p["ln1_b"]).reshape(B * S, D)
    cnt, ga, rank2, gb = _route(x1, p["peer_wq"], p["peer_keys"])
    return _peer(x1, cnt, ga, rank2, gb, p["eu"], p["evt"], p["ln2_g"], p["ln2_b"]).reshape(B, S, D)


def _trunks(xs, *weights):
    layers = [_prepare_layer(l, *weights) for l in range(DEPTH)]
    outs = []
    for x in xs:
        tables = _rotary_tables(x.shape[1])
        for p in layers:
            x = _layer(x, p, tables)
        outs.append(x)
    return tuple(outs)


def _trunk(x, *weights):
    return _trunks((x,), *weights)[0]


def kernel(x_prompt, x_sample, w_in, lambda_qk, subln_g, conv_w, conv_b, gate_a_w, gate_a_b,
           gate_i_w, gate_i_b, lru_lambda, w_out, ln1_g, ln1_b, peer_wq, peer_keys, expert_u,
           expert_v, ln2_g, ln2_b):
    return _trunks((x_prompt, x_sample), w_in, lambda_qk, subln_g, conv_w, conv_b, gate_a_w,
                   gate_a_b, gate_i_w, gate_i_b, lru_lambda, w_out, ln1_g, ln1_b, peer_wq,
                   peer_keys, expert_u, expert_v, ln2_g, ln2_b)
```
